```python
import math
import jax
import jax.numpy as jnp
from jax import lax
import numpy as np

D_MODEL = 1024
BATCH = 2
SEQ = 16384
DEPTH = 2

GRID_W = 64
CTX_LEN = 256
HEAD_DIM = 64
A_HEADS = 6
A_KV = 2
B_HEADS = 4
B_QK_DIM = HEAD_DIM // 2
C_HEADS = 6
C_KV = 2
MIX_WIDTH = (A_HEADS + B_HEADS + C_HEADS) * HEAD_DIM
IN_WIDTHS = (A_HEADS * HEAD_DIM, A_KV * HEAD_DIM, A_KV * HEAD_DIM,
             B_HEADS * HEAD_DIM, B_HEADS * HEAD_DIM, B_HEADS * HEAD_DIM,
             C_HEADS * HEAD_DIM, C_KV * HEAD_DIM, C_KV * HEAD_DIM)
IN_WIDTH = sum(IN_WIDTHS)
D_FF = -(-(8 * D_MODEL) // (3 * 256)) * 256
WINDOW = 128
Q_BLOCK = 128
ROPE_THETA = 10000.0
EPS = 1e-6
NEG_INF = -1e30

kernel_name = 'hybrid_parallel_heads_dit_block'


def rmsnorm(x, g):
    xf = x.astype(jnp.float32)
    y = xf * lax.rsqrt(jnp.mean(xf * xf, axis=-1, keepdims=True) + EPS)
    return (y * g.astype(jnp.float32)).astype(x.dtype)


def modulate(h, shift, scale):
    return h * (1 + scale) + shift


def split_columns(p):
    offsets = [int(o) for o in np.cumsum(IN_WIDTHS)[:-1]]
    return jnp.split(p, offsets, axis=-1)


def to_heads(t, n):
    b, s, _ = t.shape
    return t.reshape(b, s, n, -1).transpose(0, 2, 1, 3)


def merge_heads(o):
    b, h, s, d = o.shape
    return o.transpose(0, 2, 1, 3).reshape(b, s, h * d)


def group(q, g):
    b, h, s, d = q.shape
    return q.reshape(b, g, h // g, s, d)


def ungroup(o):
    b, g, r, s, d = o.shape
    return o.reshape(b, g * r, s, d)


def rope_1d(x, pos):
    half = x.shape[-1] // 2
    freqs = ROPE_THETA ** (-jnp.arange(half, dtype=jnp.float32) / half)
    ang = pos.astype(jnp.float32)[:, None] * freqs[None, :]
    cos = jnp.cos(ang).astype(x.dtype)
    sin = jnp.sin(ang).astype(x.dtype)
    x1, x2 = x[..., :half], x[..., half:]
    return jnp.concatenate([x1 * cos - x2 * sin, x1 * sin + x2 * cos], axis=-1)


def rope_2d(x, rows, cols):
    h = x.shape[-1] // 2
    return jnp.concatenate([rope_1d(x[..., :h], rows), rope_1d(x[..., h:], cols)], axis=-1)


def scores(q, k, scale):
    return jnp.einsum('bgrqd,bgkd->bgrqk', q, k, preferred_element_type=jnp.float32) * scale


def weigh(p, v):
    return jnp.einsum('bgrqk,bgkd->bgrqd', p.astype(v.dtype), v)


def sweep_blocks(fn, *qs):
    def split(q):
        b, g, r, s, d = q.shape
        return jnp.moveaxis(q.reshape(b, g, r, s // Q_BLOCK, Q_BLOCK, d), 3, 0)
    out = lax.map(lambda blk: fn(*blk), tuple(split(q) for q in qs))
    nb, b, g, r, qb, dv = out.shape
    return jnp.moveaxis(out, 0, 3).reshape(b, g, r, nb * qb, dv)


def window_attention(q, k, v, kc, vc, sink, scale):
    b, g, r, s, d = q.shape
    nb = s // Q_BLOCK

    def band(t):
        tp = jnp.pad(t, ((0, 0), (0, 0), (Q_BLOCK, Q_BLOCK), (0, 0)))
        tp = tp.reshape(b, g, nb + 2, Q_BLOCK, t.shape[-1])
        blocks = jnp.concatenate([tp[:, :, :-2], tp[:, :, 1:-1], tp[:, :, 2:]], axis=3)
        return jnp.moveaxis(blocks, 2, 0)

    kb, vb = band(k), band(v)
    qb = jnp.moveaxis(q.reshape(b, g, r, nb, Q_BLOCK, d), 3, 0)
    qpos = jnp.arange(nb)[:, None] * Q_BLOCK + jnp.arange(Q_BLOCK)[None, :]
    kpos = jnp.arange(nb)[:, None] * Q_BLOCK - Q_BLOCK + jnp.arange(3 * Q_BLOCK)[None, :]
    valid = ((jnp.abs(qpos[:, :, None] - kpos[:, None, :]) <= WINDOW)
             & (kpos >= 0)[:, None, :] & (kpos < s)[:, None, :])
    sink_col = jnp.broadcast_to(sink.astype(jnp.float32)[None, :, :, None, None], (b, g, r, Q_BLOCK, 1))

    def one(args):
        qblk, kblk, vblk, vmask = args
        s_loc = jnp.where(vmask, scores(qblk, kblk, scale), NEG_INF)
        s_all = jnp.concatenate([s_loc, scores(qblk, kc, scale), sink_col], axis=-1)
        p = jax.nn.softmax(s_all, axis=-1)
        return weigh(p[..., :3 * Q_BLOCK], vblk) + weigh(p[..., 3 * Q_BLOCK:-1], vc)

    out = lax.map(one, (qb, kb, vb, valid))
    return jnp.moveaxis(out, 0, 3).reshape(b, g, r, s, -1)


def diff_lambda(lam_q1, lam_k1, lam_q2, lam_k2, lam_init):
    f = lambda a, c: jnp.exp(jnp.sum(a.astype(jnp.float32) * c.astype(jnp.float32)))
    return f(lam_q1, lam_k1) - f(lam_q2, lam_k2) + lam_init


def token_mixers(hx, hc, w_in, g_q, g_k, lam_q1, lam_k1, lam_q2, lam_k2, lam_init, g_subln, sink,
                 rows, cols, with_ctx_out):
    axq, axk, axv, bxq, bxk, bxv, cxq, cxk, cxv = split_columns(hx @ w_in)
    acq, ack, acv, bcq, bck, bcv, ccq, cck, ccv = split_columns(hc @ w_in)
    pos = lambda t: rope_2d(t, rows, cols)

    sa = HEAD_DIM ** -0.5
    qa = pos(rmsnorm(to_heads(axq, A_HEADS), g_q))
    ka = pos(rmsnorm(to_heads(axk, A_KV), g_k))
    qa_c = rmsnorm(to_heads(acq, A_HEADS), g_q)
    ka_c = rmsnorm(to_heads(ack, A_KV), g_k)
    va_c = to_heads(acv, A_KV)
    ka_all = jnp.concatenate([ka, ka_c], axis=2)
    va_all = jnp.concatenate([to_heads(axv, A_KV), va_c], axis=2)
    oa = ungroup(sweep_blocks(
        lambda qblk: weigh(jax.nn.softmax(scores(qblk, ka_all, sa), axis=-1), va_all),
        group(qa, A_KV)))

    sb = B_QK_DIM ** -0.5
    lam = diff_lambda(lam_q1, lam_k1, lam_q2, lam_k2, lam_init)
    qb_x = to_heads(bxq, B_HEADS)
    kb_x = to_heads(bxk, B_HEADS)
    qb_c = to_heads(bcq, B_HEADS)
    kb_c = to_heads(bck, B_HEADS)
    vb_c = to_heads(bcv, B_HEADS)
    q1, q2 = pos(qb_x[..., :B_QK_DIM]), pos(qb_x[..., B_QK_DIM:])
    k1_all = jnp.concatenate([pos(kb_x[..., :B_QK_DIM]), kb_c[..., :B_QK_DIM]], axis=2)
    k2_all = jnp.concatenate([pos(kb_x[..., B_QK_DIM:]), kb_c[..., B_QK_DIM:]], axis=2)
    vb_all = jnp.concatenate([to_heads(bxv, B_HEADS), vb_c], axis=2)

    def diff_block(qb1, qb2):
        p = (jax.nn.softmax(scores(qb1, k1_all, sb), axis=-1)
             - lam * jax.nn.softmax(scores(qb2, k2_all, sb), axis=-1))
        return weigh(p, vb_all)

    ob = sweep_blocks(diff_block, group(q1, B_HEADS), group(q2, B_HEADS))
    ob = rmsnorm(ungroup(ob), g_subln) * (1.0 - lam_init)

    sc = HEAD_DIM ** -0.5
    sink_gr = sink.reshape(C_KV, C_HEADS // C_KV)
    qc_x = pos(to_heads(cxq, C_HEADS))
    kc_x = pos(to_heads(cxk, C_KV))
    vc_x = to_heads(cxv, C_KV)
    qc_c = to_heads(ccq, C_HEADS)
    kc_c = to_heads(cck, C_KV)
    vc_c = to_heads(ccv, C_KV)
    oc = ungroup(window_attention(group(qc_x, C_KV), kc_x, vc_x, kc_c, vc_c, sink_gr, sc))

    ox = merge_heads(jnp.concatenate([oa, ob, oc], axis=1))
    if not with_ctx_out:
        return ox, None

    oa_c = ungroup(weigh(jax.nn.softmax(scores(group(qa_c, A_KV), ka_c, sa), axis=-1), va_c))
    p_b = (jax.nn.softmax(scores(group(qb_c[..., :B_QK_DIM], B_HEADS), kb_c[..., :B_QK_DIM], sb), axis=-1)
           - lam * jax.nn.softmax(scores(group(qb_c[..., B_QK_DIM:], B_HEADS), kb_c[..., B_QK_DIM:], sb), axis=-1))
    ob_c = rmsnorm(ungroup(weigh(p_b, vb_c)), g_subln) * (1.0 - lam_init)
    s_cc = scores(group(qc_c, C_KV), kc_c, sc)
    sink_col = jnp.broadcast_to(sink_gr.astype(jnp.float32)[None, :, :, None, None], s_cc.shape[:-1] + (1,))
    p_c = jax.nn.softmax(jnp.concatenate([s_cc, sink_col], axis=-1), axis=-1)
    oc_c = ungroup(weigh(p_c[..., :-1], vc_c))
    return ox, merge_heads(jnp.concatenate([oa_c, ob_c, oc_c], axis=1))


def swiglu(h, w1, w3, w2):
    return (jax.nn.silu(h @ w1) * (h @ w3)) @ w2


def setup_inputs(seed: int = 0) -> dict:
    key = jax.random.key(seed)
    ks = jax.random.split(key, 24)
    D = D_MODEL

    def nrm(k, shape, scale):
        return jax.random.normal(k, shape, jnp.float32) * scale

    return {
        'x': nrm(ks[0], (BATCH, SEQ, D), 1.0),
        'c': nrm(ks[1], (BATCH, D), 1.0),
        'ctx': nrm(ks[2], (BATCH, CTX_LEN, D), 1.0),
        'c_ctx': nrm(ks[3], (D,), 1.0),
        'w_ada': nrm(ks[4], (DEPTH, D, 6 * D), 0.5 * D ** -0.5),
        'b_ada': nrm(ks[5], (DEPTH, 6 * D), 0.02),
        'g_attn': 1.0 + nrm(ks[6], (DEPTH, D), 0.05),
        'g_ffn': 1.0 + nrm(ks[7], (DEPTH, D), 0.05),
        'w_in': nrm(ks[8], (DEPTH, D, IN_WIDTH), D ** -0.5),
        'g_q': 1.0 + nrm(ks[9], (DEPTH, HEAD_DIM), 0.05),
        'g_k': 1.0 + nrm(ks[10], (DEPTH, HEAD_DIM), 0.05),
        'lam_q1': nrm(ks[11], (DEPTH, B_QK_DIM), 0.1),
        'lam_k1': nrm(ks[12], (DEPTH, B_QK_DIM), 0.1),
        'lam_q2': nrm(ks[13], (DEPTH, B_QK_DIM), 0.1),
        'lam_k2': nrm(ks[14], (DEPTH, B_QK_DIM), 0.1),
        'g_subln': 1.0 + nrm(ks[15], (DEPTH, HEAD_DIM), 0.05),
        'sink_logit': nrm(ks[16], (DEPTH, C_HEADS), 0.5),
        'w_out': nrm(ks[17], (DEPTH, MIX_WIDTH, D), MIX_WIDTH ** -0.5),
        'w_ff1': nrm(ks[18], (DEPTH, D, D_FF), D ** -0.5),
        'w_ff3': nrm(ks[19], (DEPTH, D, D_FF), D ** -0.5),
        'w_ff2': nrm(ks[20], (DEPTH, D_FF, D), D_FF ** -0.5),
        'g_final': 1.0 + nrm(ks[21], (D,), 0.05),
    }


def reference(x, c, ctx, c_ctx, w_ada, b_ada, g_attn, g_ffn, w_in, g_q, g_k, lam_q1, lam_k1, lam_q2, lam_k2,
              g_subln, sink_logit, w_out, w_ff1, w_ff3, w_ff2, g_final):
    n_tok = x.shape[1]
    ROWS = n_tok // GRID_W
    rows = jnp.repeat(jnp.arange(ROWS, dtype=jnp.int32), GRID_W)
    cols = jnp.tile(jnp.arange(GRID_W, dtype=jnp.int32), ROWS)
    silu_c = jax.nn.silu(c)
    silu_cc = jax.nn.silu(c_ctx)
    for layer in range(DEPTH):
        last = layer == DEPTH - 1
        lam_init = 0.8 - 0.6 * math.exp(-0.3 * layer)
        mx = jnp.split((silu_c @ w_ada[layer] + b_ada[layer])[:, None, :], 6, axis=-1)
        mc = jnp.split(silu_cc @ w_ada[layer] + b_ada[layer], 6, axis=-1)
        hx = modulate(rmsnorm(x, g_attn[layer]), mx[0], mx[1])
        hc = modulate(rmsnorm(ctx, g_attn[layer]), mc[0], mc[1])
        ox, octx = token_mixers(hx, hc, w_in[layer], g_q[layer], g_k[layer], lam_q1[layer], lam_k1[layer],
                                lam_q2[layer], lam_k2[layer], lam_init, g_subln[layer], sink_logit[layer],
                                rows, cols, not last)
        x = x + mx[2] * (ox @ w_out[layer])
        x = x + mx[5] * swiglu(modulate(rmsnorm(x, g_ffn[layer]), mx[3], mx[4]),
                               w_ff1[layer], w_ff3[layer], w_ff2[layer])
        if not last:
            ctx = ctx + mc[2] * (octx @ w_out[layer])
            ctx = ctx + mc[5] * swiglu(modulate(rmsnorm(ctx, g_ffn[layer]), mc[3], mc[4]),
                                       w_ff1[layer], w_ff3[layer], w_ff2[layer])
    return rmsnorm(x, g_final)
```

```python
import functools
import math

import jax
import jax.numpy as jnp
from jax import lax
from jax.experimental import pallas as pl
from jax.experimental.pallas import tpu as pltpu

F32 = jnp.float32
BF16 = jnp.bfloat16

GRID_W = 64
HEAD_DIM = 64
A_HEADS, A_KV = 6, 2
B_HEADS = 4
B_QK_DIM = HEAD_DIM // 2
C_HEADS, C_KV = 6, 2
WINDOW = 128
ROPE_THETA = 10000.0
EPS = 1e-6
NEG_INF = -1e30
LOG2E = 1.4426950408889634

LANES = 128
VMEM_LIMIT = 56 * 1024 * 1024
TOKEN_TILE = 256
Q_TILE = 256
K_TILE = 512


def _params(n_axes):
    return pltpu.CompilerParams(dimension_semantics=("arbitrary",) * n_axes,
                                vmem_limit_bytes=VMEM_LIMIT)


def _ada_kernel(cv_ref, w_ref, b_ref, o_ref):
    cv = cv_ref[...]
    s = cv * jax.nn.sigmoid(cv)
    o_ref[...] = jnp.dot(s, w_ref[...], preferred_element_type=F32,
                         precision=lax.Precision.HIGHEST) + b_ref[...]


def _ada_modulation(cvec, w_ada, b_ada):
    depth, d, n = w_ada.shape
    bn = n // 4
    return pl.pallas_call(
        _ada_kernel,
        grid=(depth, n // bn),
        in_specs=[pl.BlockSpec((8, d), lambda l, j: (0, 0)),
                  pl.BlockSpec((None, d, bn), lambda l, j: (l, 0, j)),
                  pl.BlockSpec((None, 1, bn), lambda l, j: (l, 0, j))],
        out_specs=pl.BlockSpec((None, 8, bn), lambda l, j: (l, 0, j)),
        out_shape=jax.ShapeDtypeStruct((depth, 8, n), F32),
        compiler_params=_params(2),
        name="ada_modulation",
    )(cvec, w_ada, b_ada.reshape(depth, 1, n))


def _rope_tables(s, c, head_dim):
    half = head_dim // 4
    t = jnp.arange(s, dtype=jnp.int32)
    rows = (t // GRID_W).astype(F32)
    cols = (t % GRID_W).astype(F32)
    freqs = ROPE_THETA ** (-jnp.arange(half, dtype=F32) / half)
    ar = rows[:, None] * freqs[None, :]
    ac = cols[:, None] * freqs[None, :]
    cos = jnp.concatenate([jnp.cos(ar)] * 2 + [jnp.cos(ac)] * 2, axis=-1)
    sin = jnp.concatenate([-jnp.sin(ar), jnp.sin(ar), -jnp.sin(ac), jnp.sin(ac)], axis=-1)
    cos = jnp.concatenate([cos, jnp.ones((c, head_dim), F32)], axis=0)
    sin = jnp.concatenate([sin, jnp.zeros((c, head_dim), F32)], axis=0)
    reps = LANES // head_dim
    return jnp.tile(cos, (1, reps)), jnp.tile(sin, (1, reps))


def _inproj_kernel(x_ref, mod_ref, g_ref, w_ref, cos64_ref, sin64_ref, cos32_ref, sin32_ref,
                   gq_ref, gk_ref,
                   qa_ref, kat_ref, va_ref, qb_ref, kbt_ref, vb_ref, qc_ref, kct_ref, vc_ref):
    t, d = x_ref.shape
    x = x_ref[...]
    mod = mod_ref[...]
    shift, scale = mod[:, 0:d], mod[:, d:2 * d]
    y = x * lax.rsqrt(jnp.mean(x * x, axis=-1, keepdims=True) + EPS) * g_ref[...]
    h = (y * (1.0 + scale) + shift).astype(BF16)
    proj = jnp.dot(h, w_ref[...], preferred_element_type=F32)

    lane = lax.broadcasted_iota(jnp.int32, (t, LANES), 1)
    low_half = lane < HEAD_DIM
    cos64, sin64 = cos64_ref[...], sin64_ref[...]
    cos32, sin32 = cos32_ref[...], sin32_ref[...]

    def slab(k):
        return proj[:, k * LANES:(k + 1) * LANES]

    def rope(v, cos, sin, half):
        fwd = pltpu.roll(v, LANES - half, 1)
        bwd = pltpu.roll(v, half, 1)
        partner = jnp.where((lane % (2 * half)) < half, fwd, bwd)
        return v * cos + partner * sin

    def head_norm(v, g):
        sq = v * v
        lo = jnp.sum(jnp.where(low_half, sq, 0.0), axis=-1, keepdims=True)
        hi = jnp.sum(jnp.where(low_half, 0.0, sq), axis=-1, keepdims=True)
        ms = jnp.where(low_half, lo, hi) * (1.0 / HEAD_DIM)
        return v * lax.rsqrt(ms + EPS) * g

    def heads_of(v):
        return v[:, 0:HEAD_DIM], v[:, HEAD_DIM:LANES]

    def with_ones(v):
        return jnp.where(low_half, v, jnp.where(lane == HEAD_DIM, 1.0, 0.0))

    def store_v(ref, idx, v):
        ref[idx] = with_ones(v).astype(BF16)
        ref[idx + 1] = with_ones(pltpu.roll(v, HEAD_DIM, 1)).astype(BF16)

    def store_kt(ref, idx, v):
        vt = v.T
        ref[idx] = vt[0:HEAD_DIM].astype(BF16)
        ref[idx + 1] = vt[HEAD_DIM:LANES].astype(BF16)

    sa = HEAD_DIM ** -0.5 * LOG2E
    for k in range(3):
        q = rope(head_norm(slab(k), gq_ref[...]), cos64, sin64, HEAD_DIM // 4) * sa
        for j, qh in enumerate(heads_of(q)):
            hd = 2 * k + j
            qa_ref[hd // 3, hd % 3] = qh.astype(BF16)
    store_kt(kat_ref, 0, rope(head_norm(slab(3), gk_ref[...]), cos64, sin64, HEAD_DIM // 4))
    store_v(va_ref, 0, slab(4))

    sb = B_QK_DIM ** -0.5 * LOG2E
    first_map = (lane % HEAD_DIM) < B_QK_DIM
    for k in range(2):
        q = rope(slab(5 + k), cos32, sin32, B_QK_DIM // 4) * sb
        q1 = jnp.where(first_map, q, 0.0)
        q2 = jnp.where(first_map, 0.0, q)
        for j in range(2):
            qb_ref[2 * k + j, 0] = heads_of(q1)[j].astype(BF16)
            qb_ref[2 * k + j, 1] = heads_of(q2)[j].astype(BF16)
        store_kt(kbt_ref, 2 * k, rope(slab(7 + k), cos32, sin32, B_QK_DIM // 4))
        store_v(vb_ref, 2 * k, slab(9 + k))

    sc = HEAD_DIM ** -0.5 * LOG2E
    for k in range(3):
        q = rope(slab(11 + k), cos64, sin64, HEAD_DIM // 4) * sc
        for j, qh in enumerate(heads_of(q)):
            hd = 2 * k + j
            qc_ref[hd // 3, hd % 3] = qh.astype(BF16)
    store_kt(kct_ref, 0, rope(slab(14), cos64, sin64, HEAD_DIM // 4))
    store_v(vc_ref, 0, slab(15))


def _input_projection(xc, mod, g_attn, w_in, tabs, g_q, g_k, n_latent):
    b, stot, d = xc.shape
    t = TOKEN_TILE
    nt = stot // t
    n_lat_tiles = n_latent // t
    n_ctx_row = mod.shape[0] - 1
    in_width = w_in.shape[1]
    cos64, sin64, cos32, sin32 = tabs
    gq = jnp.tile(g_q, 2)[None, :]
    gk = jnp.tile(g_k, 2)[None, :]

    tok = lambda bi, ti: (ti, 0)
    const2 = lambda bi, ti: (0, 0)
    qshape = lambda g, r: jax.ShapeDtypeStruct((b, g, r, stot, HEAD_DIM), BF16)
    ktshape = lambda g: jax.ShapeDtypeStruct((b, g, HEAD_DIM, stot), BF16)
    vshape = lambda g: jax.ShapeDtypeStruct((b, g, stot, LANES), BF16)
    qspec = lambda g, r: pl.BlockSpec((None, g, r, t, HEAD_DIM), lambda bi, ti: (bi, 0, 0, ti, 0))
    ktspec = lambda g: pl.BlockSpec((None, g, HEAD_DIM, t), lambda bi, ti: (bi, 0, 0, ti))
    vspec = lambda g: pl.BlockSpec((None, g, t, LANES), lambda bi, ti: (bi, 0, ti, 0))

    return pl.pallas_call(
        _inproj_kernel,
        grid=(b, nt),
        in_specs=[
            pl.BlockSpec((None, t, d), lambda bi, ti: (bi, ti, 0)),
            pl.BlockSpec((None, 1, mod.shape[-1]),
                         lambda bi, ti: (jnp.where(ti < n_lat_tiles, bi, n_ctx_row), 0, 0)),
            pl.BlockSpec((1, d), const2),
            pl.BlockSpec((d, in_width), const2),
            pl.BlockSpec((t, LANES), tok), pl.BlockSpec((t, LANES), tok),
            pl.BlockSpec((t, LANES), tok), pl.BlockSpec((t, LANES), tok),
            pl.BlockSpec((1, LANES), const2), pl.BlockSpec((1, LANES), const2),
        ],
        out_specs=[qspec(A_KV, A_HEADS // A_KV), ktspec(A_KV), vspec(A_KV),
                   qspec(B_HEADS, 2), ktspec(B_HEADS), vspec(B_HEADS),
                   qspec(C_KV, C_HEADS // C_KV), ktspec(C_KV), vspec(C_KV)],
        out_shape=[qshape(A_KV, A_HEADS // A_KV), ktshape(A_KV), vshape(A_KV),
                   qshape(B_HEADS, 2), ktshape(B_HEADS), vshape(B_HEADS),
                   qshape(C_KV, C_HEADS // C_KV), ktshape(C_KV), vshape(C_KV)],
        compiler_params=_params(2),
        name="input_projection",
    )(xc, mod, g_attn[None, :], w_in, cos64, sin64, cos32, sin32, gq, gk)


def _stream_softmax(q, kt_ref, v_ref, m_ref, acc_ref, chunks):
    m_ref[...] = jnp.full(m_ref.shape, NEG_INF, F32)
    acc_ref[...] = jnp.zeros(acc_ref.shape, F32)

    def chunk(off, size):
        s = jnp.dot(q, kt_ref[:, pl.ds(off, size)], preferred_element_type=F32)
        m_prev = m_ref[...]
        m_new = jnp.maximum(m_prev, jnp.max(s, axis=-1, keepdims=True))
        p = jnp.exp2(s - m_new).astype(BF16)
        pv = jnp.dot(p, v_ref[pl.ds(off, size), :], preferred_element_type=F32)
        acc_ref[...] = jnp.exp2(m_prev - m_new) * acc_ref[...] + pv
        m_ref[...] = m_new

    for start, count, size in chunks:
        if count == 1:
            chunk(start, size)
        else:
            def body(j, carry, start=start, size=size):
                chunk(pl.multiple_of(start + j * size, LANES), size)
                return carry
            lax.fori_loop(0, count, body, 0)


def _gqa_kernel(q_ref, kt_ref, v_ref, o_ref, m_ref, acc_ref, *, chunks):
    r, tq, dh = q_ref.shape
    _stream_softmax(q_ref[...].reshape(r * tq, dh), kt_ref, v_ref, m_ref, acc_ref, chunks)
    for i in range(r):
        a = acc_ref[i * tq:(i + 1) * tq, :]
        o_ref[i] = (a[:, 0:HEAD_DIM] / a[:, HEAD_DIM:HEAD_DIM + 1]).astype(o_ref.dtype)


def _diff_kernel(q_ref, kt_ref, v_ref, lq1_ref, lk1_ref, lq2_ref, lk2_ref, gs_ref, o_ref, m_ref, acc_ref,
                 *, chunks, lam_init):
    r, tq, dh = q_ref.shape
    _stream_softmax(q_ref[...].reshape(r * tq, dh), kt_ref, v_ref, m_ref, acc_ref, chunks)
    lam = (jnp.exp(jnp.sum(lq1_ref[...] * lk1_ref[...], axis=-1, keepdims=True))
           - jnp.exp(jnp.sum(lq2_ref[...] * lk2_ref[...], axis=-1, keepdims=True)) + lam_init)
    a1 = acc_ref[0:tq, :]
    a2 = acc_ref[tq:2 * tq, :]
    o = (a1[:, 0:HEAD_DIM] / a1[:, HEAD_DIM:HEAD_DIM + 1]
         - lam * (a2[:, 0:HEAD_DIM] / a2[:, HEAD_DIM:HEAD_DIM + 1]))
    o = o * lax.rsqrt(jnp.mean(o * o, axis=-1, keepdims=True) + EPS) * gs_ref[...]
    o_ref[...] = (o * (1.0 - lam_init)).astype(o_ref.dtype)


def _stream_attention(q, kt, v, *, q_start, n_q, tq, key_block, chunks, diff=None):
    b, g, r, _, dh = q.shape
    klen, kidx = key_block
    q0 = q_start // tq
    in_specs = [
        pl.BlockSpec((None, None, r, tq, dh), lambda bi, gi, i: (bi, gi, 0, q0 + i, 0)),
        pl.BlockSpec((None, None, dh, klen), lambda bi, gi, i: (bi, gi, 0, kidx)),
        pl.BlockSpec((None, None, klen, LANES), lambda bi, gi, i: (bi, gi, kidx, 0)),
    ]
    args = [q, kt, v]
    if diff is None:
        body = functools.partial(_gqa_kernel, chunks=chunks)
        out_shape = jax.ShapeDtypeStruct((b, g, r, n_q, HEAD_DIM), BF16)
        out_spec = pl.BlockSpec((None, None, r, tq, HEAD_DIM), lambda bi, gi, i: (bi, gi, 0, i, 0))
        name = "gqa_attention"
    else:
        lam_vecs, g_subln, lam_init = diff
        body = functools.partial(_diff_kernel, chunks=chunks, lam_init=lam_init)
        small = lambda n: pl.BlockSpec((1, n), lambda bi, gi, i: (0, 0))
        in_specs += [small(B_QK_DIM)] * 4 + [small(HEAD_DIM)]
        args += [lv[None, :] for lv in lam_vecs] + [g_subln[None, :]]
        out_shape = jax.ShapeDtypeStruct((b, g, n_q, HEAD_DIM), BF16)
        out_spec = pl.BlockSpec((None, None, tq, HEAD_DIM), lambda bi, gi, i: (bi, gi, i, 0))
        name = "diff_attention"
    return pl.pallas_call(
        body,
        grid=(b, g, n_q // tq),
        in_specs=in_specs,
        out_specs=out_spec,
        out_shape=out_shape,
        scratch_shapes=[pltpu.VMEM((r * tq, 1), F32), pltpu.VMEM((r * tq, LANES), F32)],
        compiler_params=_params(3),
        name=name,
    )(*args)


def _window_kernel(q_ref, kt_ref, v_ref, sink_ref, o_ref, *, n_latent, n_ctx, band):
    r, tq, dh = q_ref.shape
    n = r * tq
    q = q_ref[...].reshape(n, dh)
    sink = jnp.concatenate(
        [jnp.broadcast_to(sink_ref[i:i + 1, 0:1] * LOG2E, (tq, 1)) for i in range(r)], axis=0)

    s_ctx = jnp.dot(q, kt_ref[:, n_latent:n_latent + n_ctx], preferred_element_type=F32)
    m = jnp.maximum(jnp.max(s_ctx, axis=-1, keepdims=True), sink)
    if band:
        span = tq + 2 * WINDOW
        i = pl.program_id(2)
        start = pl.multiple_of(jnp.clip(i * tq - WINDOW, 0, n_latent - span), LANES)
        s_loc = jnp.dot(q, kt_ref[:, pl.ds(start, span)], preferred_element_type=F32)
        qpos = i * tq + lax.broadcasted_iota(jnp.int32, (n, span), 0) % tq
        kpos = start + lax.broadcasted_iota(jnp.int32, (n, span), 1)
        s_loc = jnp.where(jnp.abs(qpos - kpos) <= WINDOW, s_loc, NEG_INF)
        m = jnp.maximum(m, jnp.max(s_loc, axis=-1, keepdims=True))
    acc = jnp.dot(jnp.exp2(s_ctx - m).astype(BF16), v_ref[n_latent:n_latent + n_ctx, :],
                  preferred_element_type=F32)
    if band:
        acc = acc + jnp.dot(jnp.exp2(s_loc - m).astype(BF16), v_ref[pl.ds(start, span), :],
                            preferred_element_type=F32)
    denom = acc[:, HEAD_DIM:HEAD_DIM + 1] + jnp.exp2(sink - m)
    o = acc[:, 0:HEAD_DIM] / denom
    for i in range(r):
        o_ref[i] = o[i * tq:(i + 1) * tq].astype(o_ref.dtype)


def _window_attention(q, kt, v, sink_gr, *, q_start, n_q, tq, n_latent, n_ctx, band):
    b, g, r, stot, dh = q.shape
    q0 = q_start // tq
    sink = jnp.broadcast_to(sink_gr[:, :, None], (g, r, LANES)).astype(F32)
    return pl.pallas_call(
        functools.partial(_window_kernel, n_latent=n_latent, n_ctx=n_ctx, band=band),
        grid=(b, g, n_q // tq),
        in_specs=[
            pl.BlockSpec((None, None, r, tq, dh), lambda bi, gi, i: (bi, gi, 0, q0 + i, 0)),
            pl.BlockSpec((None, None, dh, stot), lambda bi, gi, i: (bi, gi, 0, 0)),
            pl.BlockSpec((None, None, stot, LANES), lambda bi, gi, i: (bi, gi, 0, 0)),
            pl.BlockSpec((None, r, LANES), lambda bi, gi, i: (gi, 0, 0)),
        ],
        out_specs=pl.BlockSpec((None, None, r, tq, HEAD_DIM), lambda bi, gi, i: (bi, gi, 0, i, 0)),
        out_shape=jax.ShapeDtypeStruct((b, g, r, n_q, HEAD_DIM), BF16),
        compiler_params=_params(3),
        name="window_attention",
    )(q, kt, v, sink)


def _post_kernel(x_ref, a_ref, mod_ref, wo_ref, g_ref, w1_ref, w3_ref, w2_ref, *rest, final):
    if final:
        gf_ref, o_ref = rest
    else:
        (o_ref,) = rest
    d = x_ref.shape[-1]
    mod = mod_ref[...]
    y = x_ref[...] + mod[:, 2 * d:3 * d] * jnp.dot(a_ref[...], wo_ref[...], preferred_element_type=F32)
    h = y * lax.rsqrt(jnp.mean(y * y, axis=-1, keepdims=True) + EPS) * g_ref[...]
    h = (h * (1.0 + mod[:, 4 * d:5 * d]) + mod[:, 3 * d:4 * d]).astype(BF16)
    u = jnp.dot(h, w1_ref[...], preferred_element_type=F32)
    v = jnp.dot(h, w3_ref[...], preferred_element_type=F32)
    act = (u * jax.nn.sigmoid(u) * v).astype(BF16)
    out = y + mod[:, 5 * d:6 * d] * jnp.dot(act, w2_ref[...], preferred_element_type=F32)
    if final:
        out = out * lax.rsqrt(jnp.mean(out * out, axis=-1, keepdims=True) + EPS) * gf_ref[...]
    o_ref[...] = out


def _post_attention(xc, attn, mod, w_out, g_ffn, w1, w3, w2, n_latent, g_final=None):
    b, stot, d = xc.shape
    t = TOKEN_TILE
    n_out = attn.shape[1]
    n_lat_tiles = n_latent // t
    n_ctx_row = mod.shape[0] - 1
    dff = w1.shape[1]
    final = g_final is not None
    const2 = lambda bi, ti: (0, 0)
    resident = lambda shape: pl.BlockSpec(shape, const2, pipeline_mode=pl.Buffered(1))
    tok3 = lambda bi, ti: (bi, ti, 0)
    in_specs = [
        pl.BlockSpec((None, t, d), tok3),
        pl.BlockSpec((None, t, attn.shape[-1]), tok3),
        pl.BlockSpec((None, 1, mod.shape[-1]),
                     lambda bi, ti: (jnp.where(ti < n_lat_tiles, bi, n_ctx_row), 0, 0)),
        resident((w_out.shape[0], d)),
        pl.BlockSpec((1, d), const2),
        resident((d, dff)), resident((d, dff)), resident((dff, d)),
    ]
    args = [xc, attn, mod, w_out, g_ffn[None, :], w1, w3, w2]
    if final:
        in_specs.append(pl.BlockSpec((1, d), const2))
        args.append(g_final[None, :])
    return pl.pallas_call(
        functools.partial(_post_kernel, final=final),
        grid=(b, n_out // t),
        in_specs=in_specs,
        out_specs=pl.BlockSpec((None, t, d), tok3),
        out_shape=jax.ShapeDtypeStruct((b, n_out, d), F32),
        compiler_params=_params(2),
        name="post_attention",
    )(*args)


def _merge(oa, ob, oc):
    b, _, _, n, dh = oa.shape
    flat = lambda o: o.reshape(b, -1, n, dh)
    o = jnp.concatenate([flat(oa), ob, flat(oc)], axis=1)
    return o.transpose(0, 2, 1, 3).reshape(b, n, -1)


def kernel(x, c, ctx, c_ctx, w_ada, b_ada, g_attn, g_ffn, w_in, g_q, g_k, lam_q1, lam_k1, lam_q2, lam_k2,
           g_subln, sink_logit, w_out, w_ff1, w_ff3, w_ff2, g_final):
    b, s, d = x.shape
    n_ctx = ctx.shape[1]
    depth = w_ada.shape[0]
    stot = s + n_ctx
    assert s % Q_TILE == 0 and s % K_TILE == 0 and s % n_ctx == 0 and n_ctx % TOKEN_TILE == 0
    assert s % GRID_W == 0 and b + 1 <= 8

    cvec = jnp.zeros((8, d), F32).at[:b].set(c).at[b].set(c_ctx)
    mods = _ada_modulation(cvec, w_ada, b_ada)[:, :b + 1, None, :]
    tabs = _rope_tables(s, n_ctx, HEAD_DIM) + _rope_tables(s, n_ctx, B_QK_DIM)

    xc = jnp.concatenate([x, ctx], axis=1)
    lat_chunks = ((0, s // K_TILE, K_TILE), (s, 1, n_ctx))
    ctx_chunks = ((0, 1, n_ctx),)
    lat = dict(q_start=0, n_q=s, tq=Q_TILE, key_block=(stot, 0), chunks=lat_chunks)
    cq = dict(q_start=s, n_q=n_ctx, tq=n_ctx, key_block=(n_ctx, s // n_ctx), chunks=ctx_chunks)

    for layer in range(depth):
        last = layer == depth - 1
        lam_init = 0.8 - 0.6 * math.exp(-0.3 * layer)
        mod = mods[layer]
        qa, kat, va, qb, kbt, vb, qc, kct, vc = _input_projection(
            xc, mod, g_attn[layer], w_in[layer].astype(BF16), tabs, g_q[layer], g_k[layer], s)
        diff = ((lam_q1[layer], lam_k1[layer], lam_q2[layer], lam_k2[layer]), g_subln[layer], lam_init)
        sink_gr = sink_logit[layer].reshape(C_KV, C_HEADS // C_KV)
        win = dict(n_latent=s, n_ctx=n_ctx)

        oa = _stream_attention(qa, kat, va, **lat)
        ob = _stream_attention(qb, kbt, vb, diff=diff, **lat)
        oc = _window_attention(qc, kct, vc, sink_gr, q_start=0, n_q=s, tq=Q_TILE, band=True, **win)
        attn = _merge(oa, ob, oc)
        if not last:
            oa_c = _stream_attention(qa, kat, va, **cq)
            ob_c = _stream_attention(qb, kbt, vb, diff=diff, **cq)
            oc_c = _window_attention(qc, kct, vc, sink_gr, q_start=s, n_q=n_ctx, tq=n_ctx, band=False, **win)
            attn = jnp.concatenate([attn, _merge(oa_c, ob_c, oc_c)], axis=1)

        xc = _post_attention(xc, attn, mod, w_out[layer].astype(BF16), g_ffn[layer],
                             w_ff1[layer].astype(BF16), w_ff3[layer].astype(BF16), w_ff2[layer].astype(BF16),
                             s, g_final if last else None)
    return xc
```

```python
import functools
import math

import jax
import jax.numpy as jnp
from jax import lax
from jax.experimental import pallas as pl
from jax.experimental.pallas import tpu as pltpu

F32 = jnp.float32
BF16 = jnp.bfloat16

GRID_W = 64
HEAD_DIM = 64
A_HEADS, A_KV = 6, 2
B_HEADS = 4
B_QK_DIM = HEAD_DIM // 2
C_HEADS, C_KV = 6, 2
WINDOW = 128
ROPE_THETA = 10000.0
EPS = 1e-6
NEG_INF = -1e30
LOG2E = 1.4426950408889634

LANES = 128
VMEM_LIMIT = 56 * 1024 * 1024
TOKEN_TILE = 256
Q_TILE = 256
K_TILE = 512


def _params(n_axes):
    return pltpu.CompilerParams(dimension_semantics=("arbitrary",) * n_axes,
                                vmem_limit_bytes=VMEM_LIMIT)


def _ada_kernel(cv_ref, w_ref, b_ref, o_ref):
    cv = cv_ref[...]
    s = cv * jax.nn.sigmoid(cv)
    o_ref[...] = jnp.dot(s, w_ref[...], preferred_element_type=F32,
                         precision=lax.Precision.HIGHEST) + b_ref[...]


def _ada_modulation(cvec, w_ada, b_ada):
    depth, d, n = w_ada.shape
    bn = n // 4
    return pl.pallas_call(
        _ada_kernel,
        grid=(depth, n // bn),
        in_specs=[pl.BlockSpec((8, d), lambda l, j: (0, 0)),
                  pl.BlockSpec((None, d, bn), lambda l, j: (l, 0, j)),
                  pl.BlockSpec((None, 1, bn), lambda l, j: (l, 0, j))],
        out_specs=pl.BlockSpec((None, 8, bn), lambda l, j: (l, 0, j)),
        out_shape=jax.ShapeDtypeStruct((depth, 8, n), F32),
        compiler_params=_params(2),
        name="ada_modulation",
    )(cvec, w_ada, b_ada.reshape(depth, 1, n))


def _rope_tables(s, c, head_dim):
    half = head_dim // 4
    t = jnp.arange(s, dtype=jnp.int32)
    rows = (t // GRID_W).astype(F32)
    cols = (t % GRID_W).astype(F32)
    freqs = ROPE_THETA ** (-jnp.arange(half, dtype=F32) / half)
    ar = rows[:, None] * freqs[None, :]
    ac = cols[:, None] * freqs[None, :]
    cos = jnp.concatenate([jnp.cos(ar)] * 2 + [jnp.cos(ac)] * 2, axis=-1)
    sin = jnp.concatenate([-jnp.sin(ar), jnp.sin(ar), -jnp.sin(ac), jnp.sin(ac)], axis=-1)
    cos = jnp.concatenate([cos, jnp.ones((c, head_dim), F32)], axis=0)
    sin = jnp.concatenate([sin, jnp.zeros((c, head_dim), F32)], axis=0)
    reps = LANES // head_dim
    return jnp.tile(cos, (1, reps)), jnp.tile(sin, (1, reps))


def _inproj_kernel(x_ref, mod_ref, g_ref, w_ref, cos64_ref, sin64_ref, cos32_ref, sin32_ref,
                   gq_ref, gk_ref,
                   qa_ref, kat_ref, va_ref, qb_ref, kbt_ref, vb_ref, qc_ref, kct_ref, vc_ref):
    t, d = x_ref.shape
    x = x_ref[...]
    mod = mod_ref[...]
    shift, scale = mod[:, 0:d], mod[:, d:2 * d]
    y = x * lax.rsqrt(jnp.mean(x * x, axis=-1, keepdims=True) + EPS) * g_ref[...]
    h = (y * (1.0 + scale) + shift).astype(BF16)
    proj = jnp.dot(h, w_ref[...], preferred_element_type=F32)

    lane = lax.broadcasted_iota(jnp.int32, (t, LANES), 1)
    low_half = lane < HEAD_DIM
    cos64, sin64 = cos64_ref[...], sin64_ref[...]
    cos32, sin32 = cos32_ref[...], sin32_ref[...]

    def slab(k):
        return proj[:, k * LANES:(k + 1) * LANES]

    def rope(v, cos, sin, half):
        fwd = pltpu.roll(v, LANES - half, 1)
        bwd = pltpu.roll(v, half, 1)
        partner = jnp.where((lane % (2 * half)) < half, fwd, bwd)
        return v * cos + partner * sin

    def head_norm(v, g):
        sq = v * v
        lo = jnp.sum(jnp.where(low_half, sq, 0.0), axis=-1, keepdims=True)
        hi = jnp.sum(jnp.where(low_half, 0.0, sq), axis=-1, keepdims=True)
        ms = jnp.where(low_half, lo, hi) * (1.0 / HEAD_DIM)
        return v * lax.rsqrt(ms + EPS) * g

    def heads_of(v):
        return v[:, 0:HEAD_DIM], v[:, HEAD_DIM:LANES]

    def with_ones(v):
        return jnp.where(low_half, v, jnp.where(lane == HEAD_DIM, 1.0, 0.0))

    def store_v(ref, idx, v):
        ref[idx] = with_ones(v).astype(BF16)
        ref[idx + 1] = with_ones(pltpu.roll(v, HEAD_DIM, 1)).astype(BF16)

    def store_kt(ref, idx, v):
        vt = v.T
        ref[idx] = vt[0:HEAD_DIM].astype(BF16)
        ref[idx + 1] = vt[HEAD_DIM:LANES].astype(BF16)

    sa = HEAD_DIM ** -0.5 * LOG2E
    for k in range(3):
        q = rope(head_norm(slab(k), gq_ref[...]), cos64, sin64, HEAD_DIM // 4) * sa
        for j, qh in enumerate(heads_of(q)):
            hd = 2 * k + j
            qa_ref[hd // 3, hd % 3] = qh.astype(BF16)
    store_kt(kat_ref, 0, rope(head_norm(slab(3), gk_ref[...]), cos64, sin64, HEAD_DIM // 4))
    store_v(va_ref, 0, slab(4))

    sb = B_QK_DIM ** -0.5 * LOG2E
    first_map = (lane % HEAD_DIM) < B_QK_DIM
    for k in range(2):
        q = rope(slab(5 + k), cos32, sin32, B_QK_DIM // 4) * sb
        q1 = jnp.where(first_map, q, 0.0)
        q2 = jnp.where(first_map, 0.0, q)
        for j in range(2):
            qb_ref[2 * k + j, 0] = heads_of(q1)[j].astype(BF16)
            qb_ref[2 * k + j, 1] = heads_of(q2)[j].astype(BF16)
        store_kt(kbt_ref, 2 * k, rope(slab(7 + k), cos32, sin32, B_QK_DIM // 4))
        store_v(vb_ref, 2 * k, slab(9 + k))

    sc = HEAD_DIM ** -0.5 * LOG2E
    for k in range(3):
        q = rope(slab(11 + k), cos64, sin64, HEAD_DIM // 4) * sc
        for j, qh in enumerate(heads_of(q)):
            hd = 2 * k + j
            qc_ref[hd // 3, hd % 3] = qh.astype(BF16)
    store_kt(kct_ref, 0, rope(slab(14), cos64, sin64, HEAD_DIM // 4))
    store_v(vc_ref, 0, slab(15))


def _input_projection(xc, mod, g_attn, w_in, tabs, g_q, g_k, n_latent):
    b, stot, d = xc.shape
    t = TOKEN_TILE
    nt = stot // t
    n_lat_tiles = n_latent // t
    n_ctx_row = mod.shape[0] - 1
    in_width = w_in.shape[1]
    cos64, sin64, cos32, sin32 = tabs
    gq = jnp.tile(g_q, 2)[None, :]
    gk = jnp.tile(g_k, 2)[None, :]

    tok = lambda bi, ti: (ti, 0)
    const2 = lambda bi, ti: (0, 0)
    qshape = lambda g, r: jax.ShapeDtypeStruct((b, g, r, stot, HEAD_DIM), BF16)
    ktshape = lambda g: jax.ShapeDtypeStruct((b, g, HEAD_DIM, stot), BF16)
    vshape = lambda g: jax.ShapeDtypeStruct((b, g, stot, LANES), BF16)
    qspec = lambda g, r: pl.BlockSpec((None, g, r, t, HEAD_DIM), lambda bi, ti: (bi, 0, 0, ti, 0))
    ktspec = lambda g: pl.BlockSpec((None, g, HEAD_DIM, t), lambda bi, ti: (bi, 0, 0, ti))
    vspec = lambda g: pl.BlockSpec((None, g, t, LANES), lambda bi, ti: (bi, 0, ti, 0))

    return pl.pallas_call(
        _inproj_kernel,
        grid=(b, nt),
        in_specs=[
            pl.BlockSpec((None, t, d), lambda bi, ti: (bi, ti, 0)),
            pl.BlockSpec((None, 1, mod.shape[-1]),
                         lambda bi, ti: (jnp.where(ti < n_lat_tiles, bi, n_ctx_row), 0, 0)),
            pl.BlockSpec((1, d), const2),
            pl.BlockSpec((d, in_width), const2),
            pl.BlockSpec((t, LANES), tok), pl.BlockSpec((t, LANES), tok),
            pl.BlockSpec((t, LANES), tok), pl.BlockSpec((t, LANES), tok),
            pl.BlockSpec((1, LANES), const2), pl.BlockSpec((1, LANES), const2),
        ],
        out_specs=[qspec(A_KV, A_HEADS // A_KV), ktspec(A_KV), vspec(A_KV),
                   qspec(B_HEADS, 2), ktspec(B_HEADS), vspec(B_HEADS),
                   qspec(C_KV, C_HEADS // C_KV), ktspec(C_KV), vspec(C_KV)],
        out_shape=[qshape(A_KV, A_HEADS // A_KV), ktshape(A_KV), vshape(A_KV),
                   qshape(B_HEADS, 2), ktshape(B_HEADS), vshape(B_HEADS),
                   qshape(C_KV, C_HEADS // C_KV), ktshape(C_KV), vshape(C_KV)],
        compiler_params=_params(2),
        name="input_projection",
    )(xc, mod, g_attn[None, :], w_in, cos64, sin64, cos32, sin32, gq, gk)


def _stream_softmax(q, kt_ref, v_ref, sa_ref, sb_ref, m_ref, acc_ref, chunks):
    n_main, tk, tail_start, tail_size = chunks
    m_ref[...] = jnp.full(m_ref.shape, NEG_INF, F32)
    acc_ref[...] = jnp.zeros(acc_ref.shape, F32)

    def scores(off, size):
        return jnp.dot(q, kt_ref[:, pl.ds(off, size)], preferred_element_type=F32)

    def absorb(s, off, size):
        m_prev = m_ref[...]
        m_new = jnp.maximum(m_prev, jnp.max(s, axis=-1, keepdims=True))
        p = jnp.concatenate([jnp.exp2(s[:, c * LANES:(c + 1) * LANES] - m_new)
                             for c in range(size // LANES)], axis=1).astype(BF16)
        pv = jnp.dot(p, v_ref[pl.ds(off, size), :], preferred_element_type=F32)
        acc_ref[...] = jnp.exp2(m_prev - m_new) * acc_ref[...] + pv
        m_ref[...] = m_new

    if n_main:
        sa_ref[...] = scores(0, tk)

        def pair(i, carry):
            off0 = pl.multiple_of(2 * i * tk, LANES)
            off1 = pl.multiple_of(off0 + tk, LANES)
            off2 = pl.multiple_of(off0 + 2 * tk, LANES)
            sb_ref[...] = scores(off1, tk)
            absorb(sa_ref[...], off0, tk)
            sa_ref[...] = scores(off2, tk)
            absorb(sb_ref[...], off1, tk)
            return carry

        lax.fori_loop(0, n_main // 2 - 1, pair, 0)
        off = (n_main - 2) * tk
        sb_ref[...] = scores(off + tk, tk)
        absorb(sa_ref[...], off, tk)
        s_tail = scores(tail_start, tail_size)
        absorb(sb_ref[...], off + tk, tk)
        absorb(s_tail, tail_start, tail_size)
    else:
        absorb(scores(tail_start, tail_size), tail_start, tail_size)


def _gqa_kernel(q_ref, kt_ref, v_ref, o_ref, sa_ref, sb_ref, m_ref, acc_ref, *, chunks):
    r, tq, dh = q_ref.shape
    _stream_softmax(q_ref[...].reshape(r * tq, dh), kt_ref, v_ref, sa_ref, sb_ref, m_ref, acc_ref, chunks)
    for i in range(r):
        a = acc_ref[i * tq:(i + 1) * tq, :]
        o_ref[i] = (a[:, 0:HEAD_DIM] / a[:, HEAD_DIM:HEAD_DIM + 1]).astype(o_ref.dtype)


def _diff_kernel(q_ref, kt_ref, v_ref, lq1_ref, lk1_ref, lq2_ref, lk2_ref, gs_ref, o_ref,
                 sa_ref, sb_ref, m_ref, acc_ref, *, chunks, lam_init):
    r, tq, dh = q_ref.shape
    _stream_softmax(q_ref[...].reshape(r * tq, dh), kt_ref, v_ref, sa_ref, sb_ref, m_ref, acc_ref, chunks)
    lam = (jnp.exp(jnp.sum(lq1_ref[...] * lk1_ref[...], axis=-1, keepdims=True))
           - jnp.exp(jnp.sum(lq2_ref[...] * lk2_ref[...], axis=-1, keepdims=True)) + lam_init)
    a1 = acc_ref[0:tq, :]
    a2 = acc_ref[tq:2 * tq, :]
    o = (a1[:, 0:HEAD_DIM] / a1[:, HEAD_DIM:HEAD_DIM + 1]
         - lam * (a2[:, 0:HEAD_DIM] / a2[:, HEAD_DIM:HEAD_DIM + 1]))
    o = o * lax.rsqrt(jnp.mean(o * o, axis=-1, keepdims=True) + EPS) * gs_ref[...]
    o_ref[...] = (o * (1.0 - lam_init)).astype(o_ref.dtype)


def _stream_attention(q, kt, v, *, q_start, n_q, tq, key_block, chunks, diff=None):
    b, g, r, _, dh = q.shape
    klen, kidx = key_block
    q0 = q_start // tq
    s_shape = (r * tq, chunks[1]) if chunks[0] else (8, LANES)
    in_specs = [
        pl.BlockSpec((None, None, r, tq, dh), lambda bi, gi, i: (bi, gi, 0, q0 + i, 0)),
        pl.BlockSpec((None, None, dh, klen), lambda bi, gi, i: (bi, gi, 0, kidx)),
        pl.BlockSpec((None, None, klen, LANES), lambda bi, gi, i: (bi, gi, kidx, 0)),
    ]
    args = [q, kt, v]
    if diff is None:
        body = functools.partial(_gqa_kernel, chunks=chunks)
        out_shape = jax.ShapeDtypeStruct((b, g, r, n_q, HEAD_DIM), BF16)
        out_spec = pl.BlockSpec((None, None, r, tq, HEAD_DIM), lambda bi, gi, i: (bi, gi, 0, i, 0))
        name = "gqa_attention"
    else:
        lam_vecs, g_subln, lam_init = diff
        body = functools.partial(_diff_kernel, chunks=chunks, lam_init=lam_init)
        small = lambda n: pl.BlockSpec((1, n), lambda bi, gi, i: (0, 0))
        in_specs += [small(B_QK_DIM)] * 4 + [small(HEAD_DIM)]
        args += [lv[None, :] for lv in lam_vecs] + [g_subln[None, :]]
        out_shape = jax.ShapeDtypeStruct((b, g, n_q, HEAD_DIM), BF16)
        out_spec = pl.BlockSpec((None, None, tq, HEAD_DIM), lambda bi, gi, i: (bi, gi, i, 0))
        name = "diff_attention"
    return pl.pallas_call(
        body,
        grid=(b, g, n_q // tq),
        in_specs=in_specs,
        out_specs=out_spec,
        out_shape=out_shape,
        scratch_shapes=[pltpu.VMEM(s_shape, F32), pltpu.VMEM(s_shape, F32),
                        pltpu.VMEM((r * tq, LANES), F32), pltpu.VMEM((r * tq, LANES), F32)],
        compiler_params=_params(3),
        name=name,
    )(*args)


def _window_kernel(q_ref, kt_ref, v_ref, sink_ref, o_ref, *, n_latent, n_ctx, band):
    r, tq, dh = q_ref.shape
    n = r * tq
    q = q_ref[...].reshape(n, dh)
    sink = jnp.concatenate(
        [jnp.broadcast_to(sink_ref[i:i + 1, 0:1] * LOG2E, (tq, 1)) for i in range(r)], axis=0)

    s_ctx = jnp.dot(q, kt_ref[:, n_latent:n_latent + n_ctx], preferred_element_type=F32)
    m = jnp.maximum(jnp.max(s_ctx, axis=-1, keepdims=True), sink)
    if band:
        span = tq + 2 * WINDOW
        i = pl.program_id(2)
        start = pl.multiple_of(jnp.clip(i * tq - WINDOW, 0, n_latent - span), LANES)
        s_loc = jnp.dot(q, kt_ref[:, pl.ds(start, span)], preferred_element_type=F32)
        qpos = i * tq + lax.broadcasted_iota(jnp.int32, (n, span), 0) % tq
        kpos = start + lax.broadcasted_iota(jnp.int32, (n, span), 1)
        s_loc = jnp.where(jnp.abs(qpos - kpos) <= WINDOW, s_loc, NEG_INF)
        m = jnp.maximum(m, jnp.max(s_loc, axis=-1, keepdims=True))
    acc = jnp.dot(jnp.exp2(s_ctx - m).astype(BF16), v_ref[n_latent:n_latent + n_ctx, :],
                  preferred_element_type=F32)
    if band:
        acc = acc + jnp.dot(jnp.exp2(s_loc - m).astype(BF16), v_ref[pl.ds(start, span), :],
                            preferred_element_type=F32)
    denom = acc[:, HEAD_DIM:HEAD_DIM + 1] + jnp.exp2(sink - m)
    o = acc[:, 0:HEAD_DIM] / denom
    for i in range(r):
        o_ref[i] = o[i * tq:(i + 1) * tq].astype(o_ref.dtype)


def _window_attention(q, kt, v, sink_gr, *, q_start, n_q, tq, n_latent, n_ctx, band):
    b, g, r, stot, dh = q.shape
    q0 = q_start // tq
    sink = jnp.broadcast_to(sink_gr[:, :, None], (g, r, LANES)).astype(F32)
    return pl.pallas_call(
        functools.partial(_window_kernel, n_latent=n_latent, n_ctx=n_ctx, band=band),
        grid=(b, g, n_q // tq),
        in_specs=[
            pl.BlockSpec((None, None, r, tq, dh), lambda bi, gi, i: (bi, gi, 0, q0 + i, 0)),
            pl.BlockSpec((None, None, dh, stot), lambda bi, gi, i: (bi, gi, 0, 0)),
            pl.BlockSpec((None, None, stot, LANES), lambda bi, gi, i: (bi, gi, 0, 0)),
            pl.BlockSpec((None, r, LANES), lambda bi, gi, i: (gi, 0, 0)),
        ],
        out_specs=pl.BlockSpec((None, None, r, tq, HEAD_DIM), lambda bi, gi, i: (bi, gi, 0, i, 0)),
        out_shape=jax.ShapeDtypeStruct((b, g, r, n_q, HEAD_DIM), BF16),
        compiler_params=_params(3),
        name="window_attention",
    )(q, kt, v, sink)


def _post_kernel(x_ref, a_ref, mod_ref, wo_ref, g_ref, w1_ref, w3_ref, w2_ref, *rest, final):
    if final:
        gf_ref, o_ref = rest
    else:
        (o_ref,) = rest
    d = x_ref.shape[-1]
    mod = mod_ref[...]
    y = x_ref[...] + mod[:, 2 * d:3 * d] * jnp.dot(a_ref[...], wo_ref[...], preferred_element_type=F32)
    h = y * lax.rsqrt(jnp.mean(y * y, axis=-1, keepdims=True) + EPS) * g_ref[...]
    h = (h * (1.0 + mod[:, 4 * d:5 * d]) + mod[:, 3 * d:4 * d]).astype(BF16)
    u = jnp.dot(h, w1_ref[...], preferred_element_type=F32)
    v = jnp.dot(h, w3_ref[...], preferred_element_type=F32)
    act = (u * jax.nn.sigmoid(u) * v).astype(BF16)
    out = y + mod[:, 5 * d:6 * d] * jnp.dot(act, w2_ref[...], preferred_element_type=F32)
    if final:
        out = out * lax.rsqrt(jnp.mean(out * out, axis=-1, keepdims=True) + EPS) * gf_ref[...]
    o_ref[...] = out


def _post_attention(xc, attn, mod, w_out, g_ffn, w1, w3, w2, n_latent, g_final=None):
    b, stot, d = xc.shape
    t = TOKEN_TILE
    n_out = attn.shape[1]
    n_lat_tiles = n_latent // t
    n_ctx_row = mod.shape[0] - 1
    dff = w1.shape[1]
    final = g_final is not None
    const2 = lambda bi, ti: (0, 0)
    resident = lambda shape: pl.BlockSpec(shape, const2, pipeline_mode=pl.Buffered(1))
    tok3 = lambda bi, ti: (bi, ti, 0)
    in_specs = [
        pl.BlockSpec((None, t, d), tok3),
        pl.BlockSpec((None, t, attn.shape[-1]), tok3),
        pl.BlockSpec((None, 1, mod.shape[-1]),
                     lambda bi, ti: (jnp.where(ti < n_lat_tiles, bi, n_ctx_row), 0, 0)),
        resident((w_out.shape[0], d)),
        pl.BlockSpec((1, d), const2),
        resident((d, dff)), resident((d, dff)), resident((dff, d)),
    ]
    args = [xc, attn, mod, w_out, g_ffn[None, :], w1, w3, w2]
    if final:
        in_specs.append(pl.BlockSpec((1, d), const2))
        args.append(g_final[None, :])
    return pl.pallas_call(
        functools.partial(_post_kernel, final=final),
        grid=(b, n_out // t),
        in_specs=in_specs,
        out_specs=pl.BlockSpec((None, t, d), tok3),
        out_shape=jax.ShapeDtypeStruct((b, n_out, d), F32),
        compiler_params=_params(2),
        name="post_attention",
    )(*args)


def _merge(oa, ob, oc):
    b, _, _, n, dh = oa.shape
    flat = lambda o: o.reshape(b, -1, n, dh)
    o = jnp.concatenate([flat(oa), ob, flat(oc)], axis=1)
    return o.transpose(0, 2, 1, 3).reshape(b, n, -1)


def kernel(x, c, ctx, c_ctx, w_ada, b_ada, g_attn, g_ffn, w_in, g_q, g_k, lam_q1, lam_k1, lam_q2, lam_k2,
           g_subln, sink_logit, w_out, w_ff1, w_ff3, w_ff2, g_final):
    b, s, d = x.shape
    n_ctx = ctx.shape[1]
    depth = w_ada.shape[0]
    stot = s + n_ctx
    assert s % Q_TILE == 0 and s % K_TILE == 0 and s % n_ctx == 0 and n_ctx % TOKEN_TILE == 0
    assert s % GRID_W == 0 and b + 1 <= 8

    cvec = jnp.zeros((8, d), F32).at[:b].set(c).at[b].set(c_ctx)
    mods = _ada_modulation(cvec, w_ada, b_ada)[:, :b + 1, None, :]
    tabs = _rope_tables(s, n_ctx, HEAD_DIM) + _rope_tables(s, n_ctx, B_QK_DIM)

    xc = jnp.concatenate([x, ctx], axis=1)
    lat_chunks = (s // K_TILE, K_TILE, s, n_ctx)
    ctx_chunks = (0, K_TILE, 0, n_ctx)
    lat = dict(q_start=0, n_q=s, tq=Q_TILE, key_block=(stot, 0), chunks=lat_chunks)
    cq = dict(q_start=s, n_q=n_ctx, tq=n_ctx, key_block=(n_ctx, s // n_ctx), chunks=ctx_chunks)

    for layer in range(depth):
        last = layer == depth - 1
        lam_init = 0.8 - 0.6 * math.exp(-0.3 * layer)
        mod = mods[layer]
        qa, kat, va, qb, kbt, vb, qc, kct, vc = _input_projection(
            xc, mod, g_attn[layer], w_in[layer].astype(BF16), tabs, g_q[layer], g_k[layer], s)
        diff = ((lam_q1[layer], lam_k1[layer], lam_q2[layer], lam_k2[layer]), g_subln[layer], lam_init)
        sink_gr = sink_logit[layer].reshape(C_KV, C_HEADS // C_KV)
        win = dict(n_latent=s, n_ctx=n_ctx)

        oa = _stream_attention(qa, kat, va, **lat)
        ob = _stream_attention(qb, kbt, vb, diff=diff, **lat)
        oc = _window_attention(qc, kct, vc, sink_gr, q_start=0, n_q=s, tq=Q_TILE, band=True, **win)
        attn = _merge(oa, ob, oc)
        if not last:
            oa_c = _stream_attention(qa, kat, va, **cq)
            ob_c = _stream_attention(qb, kbt, vb, diff=diff, **cq)
            oc_c = _window_attention(qc, kct, vc, sink_gr, q_start=s, n_q=n_ctx, tq=n_ctx, band=False, **win)
            attn = jnp.concatenate([attn, _merge(oa_c, ob_c, oc_c)], axis=1)

        xc = _post_attention(xc, attn, mod, w_out[layer].astype(BF16), g_ffn[layer],
                             w_ff1[layer].astype(BF16), w_ff3[layer].astype(BF16), w_ff2[layer].astype(BF16),
                             s, g_final if last else None)
    return xc
```

```python
import functools
import math

import jax
import jax.numpy as jnp
from jax import lax
from jax.experimental import pallas as pl
from jax.experimental.pallas import tpu as pltpu

F32 = jnp.float32
BF16 = jnp.bfloat16

GRID_W = 64
HEAD_DIM = 64
A_HEADS, A_KV = 6, 2
B_HEADS = 4
B_QK_DIM = HEAD_DIM // 2
C_HEADS, C_KV = 6, 2
WINDOW = 128
ROPE_THETA = 10000.0
EPS = 1e-6
NEG_INF = -1e30
LOG2E = 1.4426950408889634

LANES = 128
VMEM_LIMIT = 56 * 1024 * 1024
TOKEN_TILE = 256
Q_TILE = 512
Q_TILE_DIFF = 512
Q_TILE_WINDOW = 256
K_TILE = 512


def _params(n_axes):
    return pltpu.CompilerParams(dimension_semantics=("arbitrary",) * n_axes,
                                vmem_limit_bytes=VMEM_LIMIT)


def _ada_kernel(cv_ref, w_ref, b_ref, o_ref):
    cv = cv_ref[...]
    s = cv * jax.nn.sigmoid(cv)
    o_ref[...] = jnp.dot(s, w_ref[...], preferred_element_type=F32,
                         precision=lax.Precision.HIGHEST) + b_ref[...]


def _ada_modulation(cvec, w_ada, b_ada):
    depth, d, n = w_ada.shape
    bn = n // 4
    return pl.pallas_call(
        _ada_kernel,
        grid=(depth, n // bn),
        in_specs=[pl.BlockSpec((8, d), lambda l, j: (0, 0)),
                  pl.BlockSpec((None, d, bn), lambda l, j: (l, 0, j)),
                  pl.BlockSpec((None, 1, bn), lambda l, j: (l, 0, j))],
        out_specs=pl.BlockSpec((None, 8, bn), lambda l, j: (l, 0, j)),
        out_shape=jax.ShapeDtypeStruct((depth, 8, n), F32),
        compiler_params=_params(2),
        name="ada_modulation",
    )(cvec, w_ada, b_ada.reshape(depth, 1, n))


def _rope_tables(s, c, head_dim):
    half = head_dim // 4
    t = jnp.arange(s, dtype=jnp.int32)
    rows = (t // GRID_W).astype(F32)
    cols = (t % GRID_W).astype(F32)
    freqs = ROPE_THETA ** (-jnp.arange(half, dtype=F32) / half)
    ar = rows[:, None] * freqs[None, :]
    ac = cols[:, None] * freqs[None, :]
    cos = jnp.concatenate([jnp.cos(ar)] * 2 + [jnp.cos(ac)] * 2, axis=-1)
    sin = jnp.concatenate([-jnp.sin(ar), jnp.sin(ar), -jnp.sin(ac), jnp.sin(ac)], axis=-1)
    cos = jnp.concatenate([cos, jnp.ones((c, head_dim), F32)], axis=0)
    sin = jnp.concatenate([sin, jnp.zeros((c, head_dim), F32)], axis=0)
    reps = LANES // head_dim
    return jnp.tile(cos, (1, reps)), jnp.tile(sin, (1, reps))


def _inproj_kernel(x_ref, mod_ref, g_ref, w_ref, cos64_ref, sin64_ref, cos32_ref, sin32_ref,
                   gq_ref, gk_ref,
                   qa_ref, kat_ref, va_ref, qb_ref, kbt_ref, vb_ref, qc_ref, kct_ref, vc_ref):
    t, d = x_ref.shape
    x = x_ref[...]
    mod = mod_ref[...]
    shift, scale = mod[:, 0:d], mod[:, d:2 * d]
    y = x * lax.rsqrt(jnp.mean(x * x, axis=-1, keepdims=True) + EPS) * g_ref[...]
    h = (y * (1.0 + scale) + shift).astype(BF16)
    proj = jnp.dot(h, w_ref[...], preferred_element_type=F32)

    lane = lax.broadcasted_iota(jnp.int32, (t, LANES), 1)
    low_half = lane < HEAD_DIM
    cos64, sin64 = cos64_ref[...], sin64_ref[...]
    cos32, sin32 = cos32_ref[...], sin32_ref[...]

    def slab(k):
        return proj[:, k * LANES:(k + 1) * LANES]

    def rope(v, cos, sin, half):
        fwd = pltpu.roll(v, LANES - half, 1)
        bwd = pltpu.roll(v, half, 1)
        partner = jnp.where((lane % (2 * half)) < half, fwd, bwd)
        return v * cos + partner * sin

    def head_norm(v, g):
        sq = v * v
        lo = jnp.sum(jnp.where(low_half, sq, 0.0), axis=-1, keepdims=True)
        hi = jnp.sum(jnp.where(low_half, 0.0, sq), axis=-1, keepdims=True)
        ms = jnp.where(low_half, lo, hi) * (1.0 / HEAD_DIM)
        return v * lax.rsqrt(ms + EPS) * g

    def heads_of(v):
        return v[:, 0:HEAD_DIM], v[:, HEAD_DIM:LANES]

    def with_ones(v):
        return jnp.where(low_half, v, jnp.where(lane == HEAD_DIM, 1.0, 0.0))

    def store_v(ref, idx, v):
        ref[idx] = with_ones(v).astype(BF16)
        ref[idx + 1] = with_ones(pltpu.roll(v, HEAD_DIM, 1)).astype(BF16)

    def store_kt(ref, idx, v):
        vt = v.T
        ref[idx] = vt[0:HEAD_DIM].astype(BF16)
        ref[idx + 1] = vt[HEAD_DIM:LANES].astype(BF16)

    sa = HEAD_DIM ** -0.5 * LOG2E
    for k in range(3):
        q = rope(head_norm(slab(k), gq_ref[...]), cos64, sin64, HEAD_DIM // 4) * sa
        for j, qh in enumerate(heads_of(q)):
            hd = 2 * k + j
            qa_ref[hd // 3, hd % 3] = qh.astype(BF16)
    store_kt(kat_ref, 0, rope(head_norm(slab(3), gk_ref[...]), cos64, sin64, HEAD_DIM // 4))
    store_v(va_ref, 0, slab(4))

    sb = B_QK_DIM ** -0.5 * LOG2E
    first_map = (lane % HEAD_DIM) < B_QK_DIM
    for k in range(2):
        q = rope(slab(5 + k), cos32, sin32, B_QK_DIM // 4) * sb
        q1 = jnp.where(first_map, q, 0.0)
        q2 = jnp.where(first_map, 0.0, q)
        for j in range(2):
            qb_ref[2 * k + j, 0] = heads_of(q1)[j].astype(BF16)
            qb_ref[2 * k + j, 1] = heads_of(q2)[j].astype(BF16)
        store_kt(kbt_ref, 2 * k, rope(slab(7 + k), cos32, sin32, B_QK_DIM // 4))
        store_v(vb_ref, 2 * k, slab(9 + k))

    sc = HEAD_DIM ** -0.5 * LOG2E
    for k in range(3):
        q = rope(slab(11 + k), cos64, sin64, HEAD_DIM // 4) * sc
        for j, qh in enumerate(heads_of(q)):
            hd = 2 * k + j
            qc_ref[hd // 3, hd % 3] = qh.astype(BF16)
    store_kt(kct_ref, 0, rope(slab(14), cos64, sin64, HEAD_DIM // 4))
    store_v(vc_ref, 0, slab(15))


def _input_projection(xc, mod, g_attn, w_in, tabs, g_q, g_k, n_latent):
    b, stot, d = xc.shape
    t = TOKEN_TILE
    nt = stot // t
    n_lat_tiles = n_latent // t
    n_ctx_row = mod.shape[0] - 1
    in_width = w_in.shape[1]
    cos64, sin64, cos32, sin32 = tabs
    gq = jnp.tile(g_q, 2)[None, :]
    gk = jnp.tile(g_k, 2)[None, :]

    tok = lambda bi, ti: (ti, 0)
    const2 = lambda bi, ti: (0, 0)
    qshape = lambda g, r: jax.ShapeDtypeStruct((b, g, r, stot, HEAD_DIM), BF16)
    ktshape = lambda g: jax.ShapeDtypeStruct((b, g, HEAD_DIM, stot), BF16)
    vshape = lambda g: jax.ShapeDtypeStruct((b, g, stot, LANES), BF16)
    qspec = lambda g, r: pl.BlockSpec((None, g, r, t, HEAD_DIM), lambda bi, ti: (bi, 0, 0, ti, 0))
    ktspec = lambda g: pl.BlockSpec((None, g, HEAD_DIM, t), lambda bi, ti: (bi, 0, 0, ti))
    vspec = lambda g: pl.BlockSpec((None, g, t, LANES), lambda bi, ti: (bi, 0, ti, 0))

    return pl.pallas_call(
        _inproj_kernel,
        grid=(b, nt),
        in_specs=[
            pl.BlockSpec((None, t, d), lambda bi, ti: (bi, ti, 0)),
            pl.BlockSpec((None, 1, mod.shape[-1]),
                         lambda bi, ti: (jnp.where(ti < n_lat_tiles, bi, n_ctx_row), 0, 0)),
            pl.BlockSpec((1, d), const2),
            pl.BlockSpec((d, in_width), const2),
            pl.BlockSpec((t, LANES), tok), pl.BlockSpec((t, LANES), tok),
            pl.BlockSpec((t, LANES), tok), pl.BlockSpec((t, LANES), tok),
            pl.BlockSpec((1, LANES), const2), pl.BlockSpec((1, LANES), const2),
        ],
        out_specs=[qspec(A_KV, A_HEADS // A_KV), ktspec(A_KV), vspec(A_KV),
                   qspec(B_HEADS, 2), ktspec(B_HEADS), vspec(B_HEADS),
                   qspec(C_KV, C_HEADS // C_KV), ktspec(C_KV), vspec(C_KV)],
        out_shape=[qshape(A_KV, A_HEADS // A_KV), ktshape(A_KV), vshape(A_KV),
                   qshape(B_HEADS, 2), ktshape(B_HEADS), vshape(B_HEADS),
                   qshape(C_KV, C_HEADS // C_KV), ktshape(C_KV), vshape(C_KV)],
        compiler_params=_params(2),
        name="input_projection",
    )(xc, mod, g_attn[None, :], w_in, cos64, sin64, cos32, sin32, gq, gk)


def _stream_softmax(q, kt_ref, v_ref, bufs, m_ref, acc_ref, chunks):
    n_main, tk, tail_start, tail_size = chunks
    (pa_ref, ra_ref), (pb_ref, rb_ref) = bufs
    m_ref[...] = jnp.full(m_ref.shape, NEG_INF, F32)
    acc_ref[...] = jnp.zeros(acc_ref.shape, F32)

    def probs(off, size, p_ref, r_ref):
        s = jnp.dot(q, kt_ref[:, pl.ds(off, size)], preferred_element_type=F32)
        m_prev = m_ref[...]
        m_new = jnp.maximum(m_prev, jnp.max(s, axis=-1, keepdims=True))
        p_ref[:, 0:size] = jnp.concatenate(
            [jnp.exp2(s[:, c * LANES:(c + 1) * LANES] - m_new) for c in range(size // LANES)],
            axis=1).astype(BF16)
        r_ref[...] = jnp.exp2(m_prev - m_new)
        m_ref[...] = m_new

    def accumulate(off, size, p_ref, r_ref):
        pv = jnp.dot(p_ref[:, 0:size], v_ref[pl.ds(off, size), :], preferred_element_type=F32)
        acc_ref[...] = r_ref[...] * acc_ref[...] + pv

    if n_main:
        probs(0, tk, pa_ref, ra_ref)

        def pair(i, carry):
            off0 = pl.multiple_of(2 * i * tk, LANES)
            off1 = pl.multiple_of(off0 + tk, LANES)
            off2 = pl.multiple_of(off0 + 2 * tk, LANES)
            probs(off1, tk, pb_ref, rb_ref)
            accumulate(off0, tk, pa_ref, ra_ref)
            probs(off2, tk, pa_ref, ra_ref)
            accumulate(off1, tk, pb_ref, rb_ref)
            return carry

        lax.fori_loop(0, n_main // 2 - 1, pair, 0)
        off = (n_main - 2) * tk
        probs(off + tk, tk, pb_ref, rb_ref)
        accumulate(off, tk, pa_ref, ra_ref)
        probs(tail_start, tail_size, pa_ref, ra_ref)
        accumulate(off + tk, tk, pb_ref, rb_ref)
        accumulate(tail_start, tail_size, pa_ref, ra_ref)
    else:
        probs(tail_start, tail_size, pa_ref, ra_ref)
        accumulate(tail_start, tail_size, pa_ref, ra_ref)


def _gqa_kernel(q_ref, kt_ref, v_ref, o_ref, pa_ref, ra_ref, pb_ref, rb_ref, m_ref, acc_ref, *, chunks):
    r, tq, dh = q_ref.shape
    _stream_softmax(q_ref[...].reshape(r * tq, dh), kt_ref, v_ref,
                    ((pa_ref, ra_ref), (pb_ref, rb_ref)), m_ref, acc_ref, chunks)
    for i in range(r):
        a = acc_ref[i * tq:(i + 1) * tq, :]
        o_ref[i] = (a[:, 0:HEAD_DIM] / a[:, HEAD_DIM:HEAD_DIM + 1]).astype(o_ref.dtype)


def _diff_kernel(q_ref, kt_ref, v_ref, lq1_ref, lk1_ref, lq2_ref, lk2_ref, gs_ref, o_ref,
                 pa_ref, ra_ref, pb_ref, rb_ref, m_ref, acc_ref, *, chunks, lam_init):
    r, tq, dh = q_ref.shape
    _stream_softmax(q_ref[...].reshape(r * tq, dh), kt_ref, v_ref,
                    ((pa_ref, ra_ref), (pb_ref, rb_ref)), m_ref, acc_ref, chunks)
    lam = (jnp.exp(jnp.sum(lq1_ref[...] * lk1_ref[...], axis=-1, keepdims=True))
           - jnp.exp(jnp.sum(lq2_ref[...] * lk2_ref[...], axis=-1, keepdims=True)) + lam_init)
    a1 = acc_ref[0:tq, :]
    a2 = acc_ref[tq:2 * tq, :]
    o = (a1[:, 0:HEAD_DIM] / a1[:, HEAD_DIM:HEAD_DIM + 1]
         - lam * (a2[:, 0:HEAD_DIM] / a2[:, HEAD_DIM:HEAD_DIM + 1]))
    o = o * lax.rsqrt(jnp.mean(o * o, axis=-1, keepdims=True) + EPS) * gs_ref[...]
    o_ref[...] = (o * (1.0 - lam_init)).astype(o_ref.dtype)


def _stream_attention(q, kt, v, *, q_start, n_q, tq, key_block, chunks, diff=None):
    b, g, r, _, dh = q.shape
    klen, kidx = key_block
    q0 = q_start // tq
    p_shape = (r * tq, chunks[1] if chunks[0] else chunks[3])
    row_shape = (r * tq, LANES)
    in_specs = [
        pl.BlockSpec((None, None, r, tq, dh), lambda bi, gi, i: (bi, gi, 0, q0 + i, 0)),
        pl.BlockSpec((None, None, dh, klen), lambda bi, gi, i: (bi, gi, 0, kidx)),
        pl.BlockSpec((None, None, klen, LANES), lambda bi, gi, i: (bi, gi, kidx, 0)),
    ]
    args = [q, kt, v]
    if diff is None:
        body = functools.partial(_gqa_kernel, chunks=chunks)
        out_shape = jax.ShapeDtypeStruct((b, g, r, n_q, HEAD_DIM), BF16)
        out_spec = pl.BlockSpec((None, None, r, tq, HEAD_DIM), lambda bi, gi, i: (bi, gi, 0, i, 0))
        name = "gqa_attention"
    else:
        lam_vecs, g_subln, lam_init = diff
        body = functools.partial(_diff_kernel, chunks=chunks, lam_init=lam_init)
        small = lambda n: pl.BlockSpec((1, n), lambda bi, gi, i: (0, 0))
        in_specs += [small(B_QK_DIM)] * 4 + [small(HEAD_DIM)]
        args += [lv[None, :] for lv in lam_vecs] + [g_subln[None, :]]
        out_shape = jax.ShapeDtypeStruct((b, g, n_q, HEAD_DIM), BF16)
        out_spec = pl.BlockSpec((None, None, tq, HEAD_DIM), lambda bi, gi, i: (bi, gi, i, 0))
        name = "diff_attention"
    return pl.pallas_call(
        body,
        grid=(b, g, n_q // tq),
        in_specs=in_specs,
        out_specs=out_spec,
        out_shape=out_shape,
        scratch_shapes=[pltpu.VMEM(p_shape, BF16), pltpu.VMEM(row_shape, F32),
                        pltpu.VMEM(p_shape, BF16), pltpu.VMEM(row_shape, F32),
                        pltpu.VMEM(row_shape, F32), pltpu.VMEM(row_shape, F32)],
        compiler_params=_params(3),
        name=name,
    )(*args)


def _window_kernel(q_ref, kt_ref, v_ref, sink_ref, o_ref, *, n_latent, n_ctx, band):
    r, tq, dh = q_ref.shape
    n = r * tq
    q = q_ref[...].reshape(n, dh)
    sink = jnp.concatenate(
        [jnp.broadcast_to(sink_ref[i:i + 1, 0:1] * LOG2E, (tq, 1)) for i in range(r)], axis=0)

    s_ctx = jnp.dot(q, kt_ref[:, n_latent:n_latent + n_ctx], preferred_element_type=F32)
    m = jnp.maximum(jnp.max(s_ctx, axis=-1, keepdims=True), sink)
    if band:
        span = tq + 2 * WINDOW
        i = pl.program_id(2)
        start = pl.multiple_of(jnp.clip(i * tq - WINDOW, 0, n_latent - span), LANES)
        s_loc = jnp.dot(q, kt_ref[:, pl.ds(start, span)], preferred_element_type=F32)
        qpos = i * tq + lax.broadcasted_iota(jnp.int32, (n, span), 0) % tq
        kpos = start + lax.broadcasted_iota(jnp.int32, (n, span), 1)
        s_loc = jnp.where(jnp.abs(qpos - kpos) <= WINDOW, s_loc, NEG_INF)
        m = jnp.maximum(m, jnp.max(s_loc, axis=-1, keepdims=True))
    acc = jnp.dot(jnp.exp2(s_ctx - m).astype(BF16), v_ref[n_latent:n_latent + n_ctx, :],
                  preferred_element_type=F32)
    if band:
        acc = acc + jnp.dot(jnp.exp2(s_loc - m).astype(BF16), v_ref[pl.ds(start, span), :],
                            preferred_element_type=F32)
    denom = acc[:, HEAD_DIM:HEAD_DIM + 1] + jnp.exp2(sink - m)
    o = acc[:, 0:HEAD_DIM] / denom
    for i in range(r):
        o_ref[i] = o[i * tq:(i + 1) * tq].astype(o_ref.dtype)


def _window_attention(q, kt, v, sink_gr, *, q_start, n_q, tq, n_latent, n_ctx, band):
    b, g, r, stot, dh = q.shape
    q0 = q_start // tq
    sink = jnp.broadcast_to(sink_gr[:, :, None], (g, r, LANES)).astype(F32)
    return pl.pallas_call(
        functools.partial(_window_kernel, n_latent=n_latent, n_ctx=n_ctx, band=band),
        grid=(b, g, n_q // tq),
        in_specs=[
            pl.BlockSpec((None, None, r, tq, dh), lambda bi, gi, i: (bi, gi, 0, q0 + i, 0)),
            pl.BlockSpec((None, None, dh, stot), lambda bi, gi, i: (bi, gi, 0, 0)),
            pl.BlockSpec((None, None, stot, LANES), lambda bi, gi, i: (bi, gi, 0, 0)),
            pl.BlockSpec((None, r, LANES), lambda bi, gi, i: (gi, 0, 0)),
        ],
        out_specs=pl.BlockSpec((None, None, r, tq, HEAD_DIM), lambda bi, gi, i: (bi, gi, 0, i, 0)),
        out_shape=jax.ShapeDtypeStruct((b, g, r, n_q, HEAD_DIM), BF16),
        compiler_params=_params(3),
        name="window_attention",
    )(q, kt, v, sink)


def _post_kernel(x_ref, a_ref, mod_ref, wo_ref, g_ref, w1_ref, w3_ref, w2_ref, *rest, final):
    if final:
        gf_ref, o_ref = rest
    else:
        (o_ref,) = rest
    d = x_ref.shape[-1]
    mod = mod_ref[...]
    y = x_ref[...] + mod[:, 2 * d:3 * d] * jnp.dot(a_ref[...], wo_ref[...], preferred_element_type=F32)
    h = y * lax.rsqrt(jnp.mean(y * y, axis=-1, keepdims=True) + EPS) * g_ref[...]
    h = (h * (1.0 + mod[:, 4 * d:5 * d]) + mod[:, 3 * d:4 * d]).astype(BF16)
    u = jnp.dot(h, w1_ref[...], preferred_element_type=F32)
    v = jnp.dot(h, w3_ref[...], preferred_element_type=F32)
    act = (u * jax.nn.sigmoid(u) * v).astype(BF16)
    out = y + mod[:, 5 * d:6 * d] * jnp.dot(act, w2_ref[...], preferred_element_type=F32)
    if final:
        out = out * lax.rsqrt(jnp.mean(out * out, axis=-1, keepdims=True) + EPS) * gf_ref[...]
    o_ref[...] = out


def _post_attention(xc, attn, mod, w_out, g_ffn, w1, w3, w2, n_latent, g_final=None):
    b, stot, d = xc.shape
    t = TOKEN_TILE
    n_out = attn.shape[1]
    n_lat_tiles = n_latent // t
    n_ctx_row = mod.shape[0] - 1
    dff = w1.shape[1]
    final = g_final is not None
    const2 = lambda bi, ti: (0, 0)
    resident = lambda shape: pl.BlockSpec(shape, const2, pipeline_mode=pl.Buffered(1))
    tok3 = lambda bi, ti: (bi, ti, 0)
    in_specs = [
        pl.BlockSpec((None, t, d), tok3),
        pl.BlockSpec((None, t, attn.shape[-1]), tok3),
        pl.BlockSpec((None, 1, mod.shape[-1]),
                     lambda bi, ti: (jnp.where(ti < n_lat_tiles, bi, n_ctx_row), 0, 0)),
        resident((w_out.shape[0], d)),
        pl.BlockSpec((1, d), const2),
        resident((d, dff)), resident((d, dff)), resident((dff, d)),
    ]
    args = [xc, attn, mod, w_out, g_ffn[None, :], w1, w3, w2]
    if final:
        in_specs.append(pl.BlockSpec((1, d), const2))
        args.append(g_final[None, :])
    return pl.pallas_call(
        functools.partial(_post_kernel, final=final),
        grid=(b, n_out // t),
        in_specs=in_specs,
        out_specs=pl.BlockSpec((None, t, d), tok3),
        out_shape=jax.ShapeDtypeStruct((b, n_out, d), F32),
        compiler_params=_params(2),
        name="post_attention",
    )(*args)


def _merge(oa, ob, oc):
    b, _, _, n, dh = oa.shape
    flat = lambda o: o.reshape(b, -1, n, dh)
    o = jnp.concatenate([flat(oa), ob, flat(oc)], axis=1)
    return o.transpose(0, 2, 1, 3).reshape(b, n, -1)


def kernel(x, c, ctx, c_ctx, w_ada, b_ada, g_attn, g_ffn, w_in, g_q, g_k, lam_q1, lam_k1, lam_q2, lam_k2,
           g_subln, sink_logit, w_out, w_ff1, w_ff3, w_ff2, g_final):
    b, s, d = x.shape
    n_ctx = ctx.shape[1]
    depth = w_ada.shape[0]
    stot = s + n_ctx
    assert s % Q_TILE == 0 and s % K_TILE == 0 and s % n_ctx == 0 and n_ctx % TOKEN_TILE == 0
    assert s % GRID_W == 0 and b + 1 <= 8

    cvec = jnp.zeros((8, d), F32).at[:b].set(c).at[b].set(c_ctx)
    mods = _ada_modulation(cvec, w_ada, b_ada)[:, :b + 1, None, :]
    tabs = _rope_tables(s, n_ctx, HEAD_DIM) + _rope_tables(s, n_ctx, B_QK_DIM)

    xc = jnp.concatenate([x, ctx], axis=1)
    lat_chunks = (s // K_TILE, K_TILE, s, n_ctx)
    ctx_chunks = (0, K_TILE, 0, n_ctx)
    lat = dict(q_start=0, n_q=s, tq=Q_TILE, key_block=(stot, 0), chunks=lat_chunks)
    cq = dict(q_start=s, n_q=n_ctx, tq=n_ctx, key_block=(n_ctx, s // n_ctx), chunks=ctx_chunks)

    for layer in range(depth):
        last = layer == depth - 1
        lam_init = 0.8 - 0.6 * math.exp(-0.3 * layer)
        mod = mods[layer]
        qa, kat, va, qb, kbt, vb, qc, kct, vc = _input_projection(
            xc, mod, g_attn[layer], w_in[layer].astype(BF16), tabs, g_q[layer], g_k[layer], s)
        diff = ((lam_q1[layer], lam_k1[layer], lam_q2[layer], lam_k2[layer]), g_subln[layer], lam_init)
        sink_gr = sink_logit[layer].reshape(C_KV, C_HEADS // C_KV)
        win = dict(n_latent=s, n_ctx=n_ctx)

        oa = _stream_attention(qa, kat, va, **lat)
        ob = _stream_attention(qb, kbt, vb, diff=diff, **{**lat, 'tq': Q_TILE_DIFF})
        oc = _window_attention(qc, kct, vc, sink_gr, q_start=0, n_q=s, tq=Q_TILE_WINDOW, band=True, **win)
        attn = _merge(oa, ob, oc)
        if not last:
            oa_c = _stream_attention(qa, kat, va, **cq)
            ob_c = _stream_attention(qb, kbt, vb, diff=diff, **cq)
            oc_c = _window_attention(qc, kct, vc, sink_gr, q_start=s, n_q=n_ctx, tq=n_ctx, band=False, **win)
            attn = jnp.concatenate([attn, _merge(oa_c, ob_c, oc_c)], axis=1)

        xc = _post_attention(xc, attn, mod, w_out[layer].astype(BF16), g_ffn[layer],
                             w_ff1[layer].astype(BF16), w_ff3[layer].astype(BF16), w_ff2[layer].astype(BF16),
                             s, g_final if last else None)
    return xc
```

```python
import functools
import math

import jax
import jax.numpy as jnp
from jax import lax
from jax.experimental import pallas as pl
from jax.experimental.pallas import tpu as pltpu

F32 = jnp.float32
BF16 = jnp.bfloat16

GRID_W = 64
HEAD_DIM = 64
A_HEADS, A_KV = 6, 2
B_HEADS = 4
B_QK_DIM = HEAD_DIM // 2
C_HEADS, C_KV = 6, 2
WINDOW = 128
ROPE_THETA = 10000.0
EPS = 1e-6
NEG_INF = -1e30
LOG2E = 1.4426950408889634

LANES = 128
VMEM_LIMIT = 56 * 1024 * 1024
TOKEN_TILE = 256
Q_TILE = 512
Q_TILE_DIFF = 512
Q_TILE_WINDOW = 256
K_TILE = 512
CHUNK_UNROLL = 4


def _params(n_axes):
    return pltpu.CompilerParams(dimension_semantics=("arbitrary",) * n_axes,
                                vmem_limit_bytes=VMEM_LIMIT)


def _ada_kernel(cv_ref, w_ref, b_ref, o_ref):
    cv = cv_ref[...]
    s = cv * jax.nn.sigmoid(cv)
    o_ref[...] = jnp.dot(s, w_ref[...], preferred_element_type=F32,
                         precision=lax.Precision.HIGHEST) + b_ref[...]


def _ada_modulation(cvec, w_ada, b_ada):
    depth, d, n = w_ada.shape
    bn = n // 4
    return pl.pallas_call(
        _ada_kernel,
        grid=(depth, n // bn),
        in_specs=[pl.BlockSpec((8, d), lambda l, j: (0, 0)),
                  pl.BlockSpec((None, d, bn), lambda l, j: (l, 0, j)),
                  pl.BlockSpec((None, 1, bn), lambda l, j: (l, 0, j))],
        out_specs=pl.BlockSpec((None, 8, bn), lambda l, j: (l, 0, j)),
        out_shape=jax.ShapeDtypeStruct((depth, 8, n), F32),
        compiler_params=_params(2),
        name="ada_modulation",
    )(cvec, w_ada, b_ada.reshape(depth, 1, n))


def _rope_tables(s, c, head_dim):
    half = head_dim // 4
    t = jnp.arange(s, dtype=jnp.int32)
    rows = (t // GRID_W).astype(F32)
    cols = (t % GRID_W).astype(F32)
    freqs = ROPE_THETA ** (-jnp.arange(half, dtype=F32) / half)
    ar = rows[:, None] * freqs[None, :]
    ac = cols[:, None] * freqs[None, :]
    cos = jnp.concatenate([jnp.cos(ar)] * 2 + [jnp.cos(ac)] * 2, axis=-1)
    sin = jnp.concatenate([-jnp.sin(ar), jnp.sin(ar), -jnp.sin(ac), jnp.sin(ac)], axis=-1)
    cos = jnp.concatenate([cos, jnp.ones((c, head_dim), F32)], axis=0)
    sin = jnp.concatenate([sin, jnp.zeros((c, head_dim), F32)], axis=0)
    reps = LANES // head_dim
    return jnp.tile(cos, (1, reps)), jnp.tile(sin, (1, reps))


def _inproj_kernel(x_ref, mod_ref, g_ref, w_ref, cos64_ref, sin64_ref, cos32_ref, sin32_ref,
                   gq_ref, gk_ref,
                   qa_ref, kat_ref, va_ref, qb_ref, kbt_ref, vb_ref, qc_ref, kct_ref, vc_ref):
    t, d = x_ref.shape
    x = x_ref[...]
    mod = mod_ref[...]
    shift, scale = mod[:, 0:d], mod[:, d:2 * d]
    y = x * lax.rsqrt(jnp.mean(x * x, axis=-1, keepdims=True) + EPS) * g_ref[...]
    h = (y * (1.0 + scale) + shift).astype(BF16)
    proj = jnp.dot(h, w_ref[...], preferred_element_type=F32)

    lane = lax.broadcasted_iota(jnp.int32, (t, LANES), 1)
    low_half = lane < HEAD_DIM
    cos64, sin64 = cos64_ref[...], sin64_ref[...]
    cos32, sin32 = cos32_ref[...], sin32_ref[...]

    def slab(k):
        return proj[:, k * LANES:(k + 1) * LANES]

    def rope(v, cos, sin, half):
        fwd = pltpu.roll(v, LANES - half, 1)
        bwd = pltpu.roll(v, half, 1)
        partner = jnp.where((lane % (2 * half)) < half, fwd, bwd)
        return v * cos + partner * sin

    def head_norm(v, g):
        sq = v * v
        lo = jnp.sum(jnp.where(low_half, sq, 0.0), axis=-1, keepdims=True)
        hi = jnp.sum(jnp.where(low_half, 0.0, sq), axis=-1, keepdims=True)
        ms = jnp.where(low_half, lo, hi) * (1.0 / HEAD_DIM)
        return v * lax.rsqrt(ms + EPS) * g

    def heads_of(v):
        return v[:, 0:HEAD_DIM], v[:, HEAD_DIM:LANES]

    def with_ones(v):
        return jnp.where(low_half, v, jnp.where(lane == HEAD_DIM, 1.0, 0.0))

    def store_v(ref, idx, v):
        ref[idx] = with_ones(v).astype(BF16)
        ref[idx + 1] = with_ones(pltpu.roll(v, HEAD_DIM, 1)).astype(BF16)

    def store_kt(ref, idx, v):
        vt = v.T
        ref[idx] = vt[0:HEAD_DIM].astype(BF16)
        ref[idx + 1] = vt[HEAD_DIM:LANES].astype(BF16)

    sa = HEAD_DIM ** -0.5 * LOG2E
    for k in range(3):
        q = rope(head_norm(slab(k), gq_ref[...]), cos64, sin64, HEAD_DIM // 4) * sa
        for j, qh in enumerate(heads_of(q)):
            hd = 2 * k + j
            qa_ref[hd // 3, hd % 3] = qh.astype(BF16)
    store_kt(kat_ref, 0, rope(head_norm(slab(3), gk_ref[...]), cos64, sin64, HEAD_DIM // 4))
    store_v(va_ref, 0, slab(4))

    sb = B_QK_DIM ** -0.5 * LOG2E
    first_map = (lane % HEAD_DIM) < B_QK_DIM
    for k in range(2):
        q = rope(slab(5 + k), cos32, sin32, B_QK_DIM // 4) * sb
        q1 = jnp.where(first_map, q, 0.0)
        q2 = jnp.where(first_map, 0.0, q)
        for j in range(2):
            qb_ref[2 * k + j, 0] = heads_of(q1)[j].astype(BF16)
            qb_ref[2 * k + j, 1] = heads_of(q2)[j].astype(BF16)
        store_kt(kbt_ref, 2 * k, rope(slab(7 + k), cos32, sin32, B_QK_DIM // 4))
        store_v(vb_ref, 2 * k, slab(9 + k))

    sc = HEAD_DIM ** -0.5 * LOG2E
    for k in range(3):
        q = rope(slab(11 + k), cos64, sin64, HEAD_DIM // 4) * sc
        for j, qh in enumerate(heads_of(q)):
            hd = 2 * k + j
            qc_ref[hd // 3, hd % 3] = qh.astype(BF16)
    store_kt(kct_ref, 0, rope(slab(14), cos64, sin64, HEAD_DIM // 4))
    store_v(vc_ref, 0, slab(15))


def _input_projection(xc, mod, g_attn, w_in, tabs, g_q, g_k, n_latent):
    b, stot, d = xc.shape
    t = TOKEN_TILE
    nt = stot // t
    n_lat_tiles = n_latent // t
    n_ctx_row = mod.shape[0] - 1
    in_width = w_in.shape[1]
    cos64, sin64, cos32, sin32 = tabs
    gq = jnp.tile(g_q, 2)[None, :]
    gk = jnp.tile(g_k, 2)[None, :]

    tok = lambda bi, ti: (ti, 0)
    const2 = lambda bi, ti: (0, 0)
    qshape = lambda g, r: jax.ShapeDtypeStruct((b, g, r, stot, HEAD_DIM), BF16)
    ktshape = lambda g: jax.ShapeDtypeStruct((b, g, HEAD_DIM, stot), BF16)
    vshape = lambda g: jax.ShapeDtypeStruct((b, g, stot, LANES), BF16)
    qspec = lambda g, r: pl.BlockSpec((None, g, r, t, HEAD_DIM), lambda bi, ti: (bi, 0, 0, ti, 0))
    ktspec = lambda g: pl.BlockSpec((None, g, HEAD_DIM, t), lambda bi, ti: (bi, 0, 0, ti))
    vspec = lambda g: pl.BlockSpec((None, g, t, LANES), lambda bi, ti: (bi, 0, ti, 0))

    return pl.pallas_call(
        _inproj_kernel,
        grid=(b, nt),
        in_specs=[
            pl.BlockSpec((None, t, d), lambda bi, ti: (bi, ti, 0)),
            pl.BlockSpec((None, 1, mod.shape[-1]),
                         lambda bi, ti: (jnp.where(ti < n_lat_tiles, bi, n_ctx_row), 0, 0)),
            pl.BlockSpec((1, d), const2),
            pl.BlockSpec((d, in_width), const2),
            pl.BlockSpec((t, LANES), tok), pl.BlockSpec((t, LANES), tok),
            pl.BlockSpec((t, LANES), tok), pl.BlockSpec((t, LANES), tok),
            pl.BlockSpec((1, LANES), const2), pl.BlockSpec((1, LANES), const2),
        ],
        out_specs=[qspec(A_KV, A_HEADS // A_KV), ktspec(A_KV), vspec(A_KV),
                   qspec(B_HEADS, 2), ktspec(B_HEADS), vspec(B_HEADS),
                   qspec(C_KV, C_HEADS // C_KV), ktspec(C_KV), vspec(C_KV)],
        out_shape=[qshape(A_KV, A_HEADS // A_KV), ktshape(A_KV), vshape(A_KV),
                   qshape(B_HEADS, 2), ktshape(B_HEADS), vshape(B_HEADS),
                   qshape(C_KV, C_HEADS // C_KV), ktshape(C_KV), vshape(C_KV)],
        compiler_params=_params(2),
        name="input_projection",
    )(xc, mod, g_attn[None, :], w_in, cos64, sin64, cos32, sin32, gq, gk)


def _stream_softmax(q, kt_ref, v_ref, bufs, m_ref, acc_ref, chunks):
    n_main, tk, tail_start, tail_size = chunks
    m_ref[...] = jnp.full(m_ref.shape, NEG_INF, F32)
    acc_ref[...] = jnp.zeros(acc_ref.shape, F32)

    def probs(off, size, p_ref, r_ref):
        s = jnp.dot(q, kt_ref[:, pl.ds(off, size)], preferred_element_type=F32)
        m_prev = m_ref[...]
        m_new = jnp.maximum(m_prev, jnp.max(s, axis=-1, keepdims=True))
        p_ref[:, 0:size] = jnp.concatenate(
            [jnp.exp2((s[:, c * LANES:(c + 1) * LANES] - m_new).astype(BF16)) for c in range(size // LANES)],
            axis=1)
        r_ref[...] = jnp.exp2(m_prev - m_new)
        m_ref[...] = m_new

    def accumulate(off, size, p_ref, r_ref):
        pv = jnp.dot(p_ref[:, 0:size], v_ref[pl.ds(off, size), :], preferred_element_type=F32)
        acc_ref[...] = r_ref[...] * acc_ref[...] + pv

    trips = (n_main - 1) // CHUNK_UNROLL if n_main else 0
    if n_main:
        probs(0, tk, *bufs[0])

    def group(g, carry):
        base = g * (CHUNK_UNROLL * tk)
        for u in range(CHUNK_UNROLL):
            probs(pl.multiple_of(base + (u + 1) * tk, LANES), tk, *bufs[(u + 1) % 2])
            accumulate(pl.multiple_of(base + u * tk, LANES), tk, *bufs[u % 2])
        return carry

    if trips:
        lax.fori_loop(0, trips, group, 0)
    for c in range(trips * CHUNK_UNROLL, n_main):
        if c + 1 < n_main:
            probs((c + 1) * tk, tk, *bufs[(c + 1) % 2])
        else:
            probs(tail_start, tail_size, *bufs[(c + 1) % 2])
        accumulate(c * tk, tk, *bufs[c % 2])
    if not n_main:
        probs(tail_start, tail_size, *bufs[0])
    accumulate(tail_start, tail_size, *bufs[n_main % 2])


def _gqa_kernel(q_ref, kt_ref, v_ref, o_ref, pa_ref, ra_ref, pb_ref, rb_ref, m_ref, acc_ref, *, chunks):
    r, tq, dh = q_ref.shape
    _stream_softmax(q_ref[...].reshape(r * tq, dh), kt_ref, v_ref,
                    ((pa_ref, ra_ref), (pb_ref, rb_ref)), m_ref, acc_ref, chunks)
    for i in range(r):
        a = acc_ref[i * tq:(i + 1) * tq, :]
        o_ref[i] = (a[:, 0:HEAD_DIM] / a[:, HEAD_DIM:HEAD_DIM + 1]).astype(o_ref.dtype)


def _diff_kernel(q_ref, kt_ref, v_ref, lq1_ref, lk1_ref, lq2_ref, lk2_ref, gs_ref, o_ref,
                 pa_ref, ra_ref, pb_ref, rb_ref, m_ref, acc_ref, *, chunks, lam_init):
    r, tq, dh = q_ref.shape
    _stream_softmax(q_ref[...].reshape(r * tq, dh), kt_ref, v_ref,
                    ((pa_ref, ra_ref), (pb_ref, rb_ref)), m_ref, acc_ref, chunks)
    lam = (jnp.exp(jnp.sum(lq1_ref[...] * lk1_ref[...], axis=-1, keepdims=True))
           - jnp.exp(jnp.sum(lq2_ref[...] * lk2_ref[...], axis=-1, keepdims=True)) + lam_init)
    a1 = acc_ref[0:tq, :]
    a2 = acc_ref[tq:2 * tq, :]
    o = (a1[:, 0:HEAD_DIM] / a1[:, HEAD_DIM:HEAD_DIM + 1]
         - lam * (a2[:, 0:HEAD_DIM] / a2[:, HEAD_DIM:HEAD_DIM + 1]))
    o = o * lax.rsqrt(jnp.mean(o * o, axis=-1, keepdims=True) + EPS) * gs_ref[...]
    o_ref[...] = (o * (1.0 - lam_init)).astype(o_ref.dtype)


def _stream_attention(q, kt, v, *, q_start, n_q, tq, key_block, chunks, diff=None):
    b, g, r, _, dh = q.shape
    klen, kidx = key_block
    q0 = q_start // tq
    p_shape = (r * tq, chunks[1] if chunks[0] else chunks[3])
    row_shape = (r * tq, LANES)
    in_specs = [
        pl.BlockSpec((None, None, r, tq, dh), lambda bi, gi, i: (bi, gi, 0, q0 + i, 0)),
        pl.BlockSpec((None, None, dh, klen), lambda bi, gi, i: (bi, gi, 0, kidx)),
        pl.BlockSpec((None, None, klen, LANES), lambda bi, gi, i: (bi, gi, kidx, 0)),
    ]
    args = [q, kt, v]
    if diff is None:
        body = functools.partial(_gqa_kernel, chunks=chunks)
        out_shape = jax.ShapeDtypeStruct((b, g, r, n_q, HEAD_DIM), BF16)
        out_spec = pl.BlockSpec((None, None, r, tq, HEAD_DIM), lambda bi, gi, i: (bi, gi, 0, i, 0))
        name = "gqa_attention"
    else:
        lam_vecs, g_subln, lam_init = diff
        body = functools.partial(_diff_kernel, chunks=chunks, lam_init=lam_init)
        small = lambda n: pl.BlockSpec((1, n), lambda bi, gi, i: (0, 0))
        in_specs += [small(B_QK_DIM)] * 4 + [small(HEAD_DIM)]
        args += [lv[None, :] for lv in lam_vecs] + [g_subln[None, :]]
        out_shape = jax.ShapeDtypeStruct((b, g, n_q, HEAD_DIM), BF16)
        out_spec = pl.BlockSpec((None, None, tq, HEAD_DIM), lambda bi, gi, i: (bi, gi, i, 0))
        name = "diff_attention"
    return pl.pallas_call(
        body,
        grid=(b, g, n_q // tq),
        in_specs=in_specs,
        out_specs=out_spec,
        out_shape=out_shape,
        scratch_shapes=[pltpu.VMEM(p_shape, BF16), pltpu.VMEM(row_shape, F32),
                        pltpu.VMEM(p_shape, BF16), pltpu.VMEM(row_shape, F32),
                        pltpu.VMEM(row_shape, F32), pltpu.VMEM(row_shape, F32)],
        compiler_params=_params(3),
        name=name,
    )(*args)


def _window_kernel(q_ref, kt_ref, v_ref, sink_ref, o_ref, *, n_latent, n_ctx, band):
    r, tq, dh = q_ref.shape
    n = r * tq
    q = q_ref[...].reshape(n, dh)
    sink = jnp.concatenate(
        [jnp.broadcast_to(sink_ref[i:i + 1, 0:1] * LOG2E, (tq, 1)) for i in range(r)], axis=0)

    s_ctx = jnp.dot(q, kt_ref[:, n_latent:n_latent + n_ctx], preferred_element_type=F32)
    m = jnp.maximum(jnp.max(s_ctx, axis=-1, keepdims=True), sink)
    if band:
        span = tq + 2 * WINDOW
        i = pl.program_id(2)
        start = pl.multiple_of(jnp.clip(i * tq - WINDOW, 0, n_latent - span), LANES)
        s_loc = jnp.dot(q, kt_ref[:, pl.ds(start, span)], preferred_element_type=F32)
        qpos = i * tq + lax.broadcasted_iota(jnp.int32, (n, span), 0) % tq
        kpos = start + lax.broadcasted_iota(jnp.int32, (n, span), 1)
        s_loc = jnp.where(jnp.abs(qpos - kpos) <= WINDOW, s_loc, NEG_INF)
        m = jnp.maximum(m, jnp.max(s_loc, axis=-1, keepdims=True))
    acc = jnp.dot(jnp.exp2(s_ctx - m).astype(BF16), v_ref[n_latent:n_latent + n_ctx, :],
                  preferred_element_type=F32)
    if band:
        acc = acc + jnp.dot(jnp.exp2(s_loc - m).astype(BF16), v_ref[pl.ds(start, span), :],
                            preferred_element_type=F32)
    denom = acc[:, HEAD_DIM:HEAD_DIM + 1] + jnp.exp2(sink - m)
    o = acc[:, 0:HEAD_DIM] / denom
    for i in range(r):
        o_ref[i] = o[i * tq:(i + 1) * tq].astype(o_ref.dtype)


def _window_attention(q, kt, v, sink_gr, *, q_start, n_q, tq, n_latent, n_ctx, band):
    b, g, r, stot, dh = q.shape
    q0 = q_start // tq
    sink = jnp.broadcast_to(sink_gr[:, :, None], (g, r, LANES)).astype(F32)
    return pl.pallas_call(
        functools.partial(_window_kernel, n_latent=n_latent, n_ctx=n_ctx, band=band),
        grid=(b, g, n_q // tq),
        in_specs=[
            pl.BlockSpec((None, None, r, tq, dh), lambda bi, gi, i: (bi, gi, 0, q0 + i, 0)),
            pl.BlockSpec((None, None, dh, stot), lambda bi, gi, i: (bi, gi, 0, 0)),
            pl.BlockSpec((None, None, stot, LANES), lambda bi, gi, i: (bi, gi, 0, 0)),
            pl.BlockSpec((None, r, LANES), lambda bi, gi, i: (gi, 0, 0)),
        ],
        out_specs=pl.BlockSpec((None, None, r, tq, HEAD_DIM), lambda bi, gi, i: (bi, gi, 0, i, 0)),
        out_shape=jax.ShapeDtypeStruct((b, g, r, n_q, HEAD_DIM), BF16),
        compiler_params=_params(3),
        name="window_attention",
    )(q, kt, v, sink)


def _post_kernel(x_ref, a_ref, mod_ref, wo_ref, g_ref, w1_ref, w3_ref, w2_ref, *rest, final):
    if final:
        gf_ref, o_ref = rest
    else:
        (o_ref,) = rest
    d = x_ref.shape[-1]
    mod = mod_ref[...]
    y = x_ref[...] + mod[:, 2 * d:3 * d] * jnp.dot(a_ref[...], wo_ref[...], preferred_element_type=F32)
    h = y * lax.rsqrt(jnp.mean(y * y, axis=-1, keepdims=True) + EPS) * g_ref[...]
    h = (h * (1.0 + mod[:, 4 * d:5 * d]) + mod[:, 3 * d:4 * d]).astype(BF16)
    u = jnp.dot(h, w1_ref[...], preferred_element_type=F32)
    v = jnp.dot(h, w3_ref[...], preferred_element_type=F32)
    act = (u * jax.nn.sigmoid(u) * v).astype(BF16)
    out = y + mod[:, 5 * d:6 * d] * jnp.dot(act, w2_ref[...], preferred_element_type=F32)
    if final:
        out = out * lax.rsqrt(jnp.mean(out * out, axis=-1, keepdims=True) + EPS) * gf_ref[...]
    o_ref[...] = out


def _post_attention(xc, attn, mod, w_out, g_ffn, w1, w3, w2, n_latent, g_final=None):
    b, stot, d = xc.shape
    t = TOKEN_TILE
    n_out = attn.shape[1]
    n_lat_tiles = n_latent // t
    n_ctx_row = mod.shape[0] - 1
    dff = w1.shape[1]
    final = g_final is not None
    const2 = lambda bi, ti: (0, 0)
    resident = lambda shape: pl.BlockSpec(shape, const2, pipeline_mode=pl.Buffered(1))
    tok3 = lambda bi, ti: (bi, ti, 0)
    in_specs = [
        pl.BlockSpec((None, t, d), tok3),
        pl.BlockSpec((None, t, attn.shape[-1]), tok3),
        pl.BlockSpec((None, 1, mod.shape[-1]),
                     lambda bi, ti: (jnp.where(ti < n_lat_tiles, bi, n_ctx_row), 0, 0)),
        resident((w_out.shape[0], d)),
        pl.BlockSpec((1, d), const2),
        resident((d, dff)), resident((d, dff)), resident((dff, d)),
    ]
    args = [xc, attn, mod, w_out, g_ffn[None, :], w1, w3, w2]
    if final:
        in_specs.append(pl.BlockSpec((1, d), const2))
        args.append(g_final[None, :])
    return pl.pallas_call(
        functools.partial(_post_kernel, final=final),
        grid=(b, n_out // t),
        in_specs=in_specs,
        out_specs=pl.BlockSpec((None, t, d), tok3),
        out_shape=jax.ShapeDtypeStruct((b, n_out, d), F32),
        compiler_params=_params(2),
        name="post_attention",
    )(*args)


def _merge(oa, ob, oc):
    b, _, _, n, dh = oa.shape
    flat = lambda o: o.reshape(b, -1, n, dh)
    o = jnp.concatenate([flat(oa), ob, flat(oc)], axis=1)
    return o.transpose(0, 2, 1, 3).reshape(b, n, -1)


def kernel(x, c, ctx, c_ctx, w_ada, b_ada, g_attn, g_ffn, w_in, g_q, g_k, lam_q1, lam_k1, lam_q2, lam_k2,
           g_subln, sink_logit, w_out, w_ff1, w_ff3, w_ff2, g_final):
    b, s, d = x.shape
    n_ctx = ctx.shape[1]
    depth = w_ada.shape[0]
    stot = s + n_ctx
    assert s % Q_TILE == 0 and s % K_TILE == 0 and s % n_ctx == 0 and n_ctx % TOKEN_TILE == 0
    assert s % GRID_W == 0 and b + 1 <= 8

    cvec = jnp.zeros((8, d), F32).at[:b].set(c).at[b].set(c_ctx)
    mods = _ada_modulation(cvec, w_ada, b_ada)[:, :b + 1, None, :]
    tabs = _rope_tables(s, n_ctx, HEAD_DIM) + _rope_tables(s, n_ctx, B_QK_DIM)

    xc = jnp.concatenate([x, ctx], axis=1)
    lat_chunks = (s // K_TILE, K_TILE, s, n_ctx)
    ctx_chunks = (0, K_TILE, 0, n_ctx)
    lat = dict(q_start=0, n_q=s, tq=Q_TILE, key_block=(stot, 0), chunks=lat_chunks)
    cq = dict(q_start=s, n_q=n_ctx, tq=n_ctx, key_block=(n_ctx, s // n_ctx), chunks=ctx_chunks)

    for layer in range(depth):
        last = layer == depth - 1
        lam_init = 0.8 - 0.6 * math.exp(-0.3 * layer)
        mod = mods[layer]
        qa, kat, va, qb, kbt, vb, qc, kct, vc = _input_projection(
            xc, mod, g_attn[layer], w_in[layer].astype(BF16), tabs, g_q[layer], g_k[layer], s)
        diff = ((lam_q1[layer], lam_k1[layer], lam_q2[layer], lam_k2[layer]), g_subln[layer], lam_init)
        sink_gr = sink_logit[layer].reshape(C_KV, C_HEADS // C_KV)
        win = dict(n_latent=s, n_ctx=n_ctx)

        oa = _stream_attention(qa, kat, va, **lat)
        ob = _stream_attention(qb, kbt, vb, diff=diff, **{**lat, 'tq': Q_TILE_DIFF})
        oc = _window_attention(qc, kct, vc, sink_gr, q_start=0, n_q=s, tq=Q_TILE_WINDOW, band=True, **win)
        attn = _merge(oa, ob, oc)
        if not last:
            oa_c = _stream_attention(qa, kat, va, **cq)
            ob_c = _stream_attention(qb, kbt, vb, diff=diff, **cq)
            oc_c = _window_attention(qc, kct, vc, sink_gr, q_start=s, n_q=n_ctx, tq=n_ctx, band=False, **win)
            attn = jnp.concatenate([attn, _merge(oa_c, ob_c, oc_c)], axis=1)

        xc = _post_attention(xc, attn, mod, w_out[layer].astype(BF16), g_ffn[layer],
                             w_ff1[layer].astype(BF16), w_ff3[layer].astype(BF16), w_ff2[layer].astype(BF16),
                             s, g_final if last else None)
    return xc
```

```python
import functools
import math

import jax
import jax.numpy as jnp
from jax import lax
from jax.experimental import pallas as pl
from jax.experimental.pallas import tpu as pltpu

F32 = jnp.float32
BF16 = jnp.bfloat16

GRID_W = 64
HEAD_DIM = 64
A_HEADS, A_KV = 6, 2
B_HEADS = 4
B_QK_DIM = HEAD_DIM // 2
C_HEADS, C_KV = 6, 2
WINDOW = 128
ROPE_THETA = 10000.0
EPS = 1e-6
NEG_INF = -1e30
LOG2E = 1.4426950408889634

LANES = 128
VMEM_LIMIT = 56 * 1024 * 1024
TOKEN_TILE = 256
Q_TILE = 1024
Q_TILE_DIFF = 512
Q_TILE_WINDOW = 256
K_TILE = 256
CHUNK_UNROLL = 16
CHUNK_UNROLL_DIFF = 32


def _params(n_axes):
    return pltpu.CompilerParams(dimension_semantics=("arbitrary",) * n_axes,
                                vmem_limit_bytes=VMEM_LIMIT)


def _ada_kernel(cv_ref, w_ref, b_ref, o_ref):
    cv = cv_ref[...]
    s = cv * jax.nn.sigmoid(cv)
    o_ref[...] = jnp.dot(s, w_ref[...], preferred_element_type=F32,
                         precision=lax.Precision.HIGHEST) + b_ref[...]


def _ada_modulation(cvec, w_ada, b_ada):
    depth, d, n = w_ada.shape
    bn = n // 4
    return pl.pallas_call(
        _ada_kernel,
        grid=(depth, n // bn),
        in_specs=[pl.BlockSpec((8, d), lambda l, j: (0, 0)),
                  pl.BlockSpec((None, d, bn), lambda l, j: (l, 0, j)),
                  pl.BlockSpec((None, 1, bn), lambda l, j: (l, 0, j))],
        out_specs=pl.BlockSpec((None, 8, bn), lambda l, j: (l, 0, j)),
        out_shape=jax.ShapeDtypeStruct((depth, 8, n), F32),
        compiler_params=_params(2),
        name="ada_modulation",
    )(cvec, w_ada, b_ada.reshape(depth, 1, n))


def _rope_tables(s, c, head_dim):
    half = head_dim // 4
    t = jnp.arange(s, dtype=jnp.int32)
    rows = (t // GRID_W).astype(F32)
    cols = (t % GRID_W).astype(F32)
    freqs = ROPE_THETA ** (-jnp.arange(half, dtype=F32) / half)
    ar = rows[:, None] * freqs[None, :]
    ac = cols[:, None] * freqs[None, :]
    cos = jnp.concatenate([jnp.cos(ar)] * 2 + [jnp.cos(ac)] * 2, axis=-1)
    sin = jnp.concatenate([-jnp.sin(ar), jnp.sin(ar), -jnp.sin(ac), jnp.sin(ac)], axis=-1)
    cos = jnp.concatenate([cos, jnp.ones((c, head_dim), F32)], axis=0)
    sin = jnp.concatenate([sin, jnp.zeros((c, head_dim), F32)], axis=0)
    reps = LANES // head_dim
    return jnp.tile(cos, (1, reps)), jnp.tile(sin, (1, reps))


def _inproj_kernel(x_ref, mod_ref, g_ref, w_ref, cos64_ref, sin64_ref, cos32_ref, sin32_ref,
                   gq_ref, gk_ref,
                   qa_ref, kat_ref, va_ref, qb_ref, kbt_ref, vb_ref, qc_ref, kct_ref, vc_ref):
    t, d = x_ref.shape
    x = x_ref[...]
    mod = mod_ref[...]
    shift, scale = mod[:, 0:d], mod[:, d:2 * d]
    y = x * lax.rsqrt(jnp.mean(x * x, axis=-1, keepdims=True) + EPS) * g_ref[...]
    h = (y * (1.0 + scale) + shift).astype(BF16)
    proj = jnp.dot(h, w_ref[...], preferred_element_type=F32)

    lane = lax.broadcasted_iota(jnp.int32, (t, LANES), 1)
    low_half = lane < HEAD_DIM
    cos64, sin64 = cos64_ref[...], sin64_ref[...]
    cos32, sin32 = cos32_ref[...], sin32_ref[...]

    def slab(k):
        return proj[:, k * LANES:(k + 1) * LANES]

    def rope(v, cos, sin, half):
        fwd = pltpu.roll(v, LANES - half, 1)
        bwd = pltpu.roll(v, half, 1)
        partner = jnp.where((lane % (2 * half)) < half, fwd, bwd)
        return v * cos + partner * sin

    def head_norm(v, g):
        sq = v * v
        lo = jnp.sum(jnp.where(low_half, sq, 0.0), axis=-1, keepdims=True)
        hi = jnp.sum(jnp.where(low_half, 0.0, sq), axis=-1, keepdims=True)
        ms = jnp.where(low_half, lo, hi) * (1.0 / HEAD_DIM)
        return v * lax.rsqrt(ms + EPS) * g

    def heads_of(v):
        return v[:, 0:HEAD_DIM], v[:, HEAD_DIM:LANES]

    def with_ones(v):
        return jnp.where(low_half, v, jnp.where(lane == HEAD_DIM, 1.0, 0.0))

    def store_v(ref, idx, v):
        ref[idx] = with_ones(v).astype(BF16)
        ref[idx + 1] = with_ones(pltpu.roll(v, HEAD_DIM, 1)).astype(BF16)

    def store_kt(ref, idx, v):
        vt = v.T
        ref[idx] = vt[0:HEAD_DIM].astype(BF16)
        ref[idx + 1] = vt[HEAD_DIM:LANES].astype(BF16)

    sa = HEAD_DIM ** -0.5 * LOG2E
    for k in range(3):
        q = rope(head_norm(slab(k), gq_ref[...]), cos64, sin64, HEAD_DIM // 4) * sa
        for j, qh in enumerate(heads_of(q)):
            hd = 2 * k + j
            qa_ref[hd // 3, hd % 3] = qh.astype(BF16)
    store_kt(kat_ref, 0, rope(head_norm(slab(3), gk_ref[...]), cos64, sin64, HEAD_DIM // 4))
    store_v(va_ref, 0, slab(4))

    sb = B_QK_DIM ** -0.5 * LOG2E
    first_map = (lane % HEAD_DIM) < B_QK_DIM
    for k in range(2):
        q = rope(slab(5 + k), cos32, sin32, B_QK_DIM // 4) * sb
        q1 = jnp.where(first_map, q, 0.0)
        q2 = jnp.where(first_map, 0.0, q)
        for j in range(2):
            qb_ref[2 * k + j, 0] = heads_of(q1)[j].astype(BF16)
            qb_ref[2 * k + j, 1] = heads_of(q2)[j].astype(BF16)
        store_kt(kbt_ref, 2 * k, rope(slab(7 + k), cos32, sin32, B_QK_DIM // 4))
        store_v(vb_ref, 2 * k, slab(9 + k))

    sc = HEAD_DIM ** -0.5 * LOG2E
    for k in range(3):
        q = rope(slab(11 + k), cos64, sin64, HEAD_DIM // 4) * sc
        for j, qh in enumerate(heads_of(q)):
            hd = 2 * k + j
            qc_ref[hd // 3, hd % 3] = qh.astype(BF16)
    store_kt(kct_ref, 0, rope(slab(14), cos64, sin64, HEAD_DIM // 4))
    store_v(vc_ref, 0, slab(15))


def _input_projection(xc, mod, g_attn, w_in, tabs, g_q, g_k, n_latent):
    b, stot, d = xc.shape
    t = TOKEN_TILE
    nt = stot // t
    n_lat_tiles = n_latent // t
    n_ctx_row = mod.shape[0] - 1
    in_width = w_in.shape[1]
    cos64, sin64, cos32, sin32 = tabs
    gq = jnp.tile(g_q, 2)[None, :]
    gk = jnp.tile(g_k, 2)[None, :]

    tok = lambda bi, ti: (ti, 0)
    const2 = lambda bi, ti: (0, 0)
    qshape = lambda g, r: jax.ShapeDtypeStruct((b, g, r, stot, HEAD_DIM), BF16)
    ktshape = lambda g: jax.ShapeDtypeStruct((b, g, HEAD_DIM, stot), BF16)
    vshape = lambda g: jax.ShapeDtypeStruct((b, g, stot, LANES), BF16)
    qspec = lambda g, r: pl.BlockSpec((None, g, r, t, HEAD_DIM), lambda bi, ti: (bi, 0, 0, ti, 0))
    ktspec = lambda g: pl.BlockSpec((None, g, HEAD_DIM, t), lambda bi, ti: (bi, 0, 0, ti))
    vspec = lambda g: pl.BlockSpec((None, g, t, LANES), lambda bi, ti: (bi, 0, ti, 0))

    return pl.pallas_call(
        _inproj_kernel,
        grid=(b, nt),
        in_specs=[
            pl.BlockSpec((None, t, d), lambda bi, ti: (bi, ti, 0)),
            pl.BlockSpec((None, 1, mod.shape[-1]),
                         lambda bi, ti: (jnp.where(ti < n_lat_tiles, bi, n_ctx_row), 0, 0)),
            pl.BlockSpec((1, d), const2),
            pl.BlockSpec((d, in_width), const2),
            pl.BlockSpec((t, LANES), tok), pl.BlockSpec((t, LANES), tok),
            pl.BlockSpec((t, LANES), tok), pl.BlockSpec((t, LANES), tok),
            pl.BlockSpec((1, LANES), const2), pl.BlockSpec((1, LANES), const2),
        ],
        out_specs=[qspec(A_KV, A_HEADS // A_KV), ktspec(A_KV), vspec(A_KV),
                   qspec(B_HEADS, 2), ktspec(B_HEADS), vspec(B_HEADS),
                   qspec(C_KV, C_HEADS // C_KV), ktspec(C_KV), vspec(C_KV)],
        out_shape=[qshape(A_KV, A_HEADS // A_KV), ktshape(A_KV), vshape(A_KV),
                   qshape(B_HEADS, 2), ktshape(B_HEADS), vshape(B_HEADS),
                   qshape(C_KV, C_HEADS // C_KV), ktshape(C_KV), vshape(C_KV)],
        compiler_params=_params(2),
        name="input_projection",
    )(xc, mod, g_attn[None, :], w_in, cos64, sin64, cos32, sin32, gq, gk)


def _stream_softmax(q, kt_ref, v_ref, bufs, m_ref, acc_ref, chunks, unroll):
    n_main, tk, tail_start, tail_size = chunks
    m_ref[...] = jnp.full(m_ref.shape, NEG_INF, F32)
    acc_ref[...] = jnp.zeros(acc_ref.shape, F32)

    def probs(off, size, p_ref, r_ref):
        s = jnp.dot(q, kt_ref[:, pl.ds(off, size)], preferred_element_type=F32)
        m_prev = m_ref[...]
        m_new = jnp.maximum(m_prev, jnp.max(s, axis=-1, keepdims=True))
        p_ref[:, 0:size] = jnp.concatenate(
            [jnp.exp2(s[:, c * LANES:(c + 1) * LANES] - m_new) for c in range(size // LANES)],
            axis=1).astype(BF16)
        r_ref[...] = jnp.exp2(m_prev - m_new)
        m_ref[...] = m_new

    def accumulate(off, size, p_ref, r_ref):
        pv = jnp.dot(p_ref[:, 0:size], v_ref[pl.ds(off, size), :], preferred_element_type=F32)
        acc_ref[...] = r_ref[...] * acc_ref[...] + pv

    trips = (n_main - 1) // unroll if n_main else 0
    if n_main:
        probs(0, tk, *bufs[0])

    def group(g, carry):
        base = g * (unroll * tk)
        for u in range(unroll):
            probs(pl.multiple_of(base + (u + 1) * tk, LANES), tk, *bufs[(u + 1) % 2])
            accumulate(pl.multiple_of(base + u * tk, LANES), tk, *bufs[u % 2])
        return carry

    if trips:
        lax.fori_loop(0, trips, group, 0)
    for c in range(trips * unroll, n_main):
        if c + 1 < n_main:
            probs((c + 1) * tk, tk, *bufs[(c + 1) % 2])
        else:
            probs(tail_start, tail_size, *bufs[(c + 1) % 2])
        accumulate(c * tk, tk, *bufs[c % 2])
    if not n_main:
        probs(tail_start, tail_size, *bufs[0])
    accumulate(tail_start, tail_size, *bufs[n_main % 2])


def _gqa_kernel(q_ref, kt_ref, v_ref, o_ref, pa_ref, ra_ref, pb_ref, rb_ref, m_ref, acc_ref, *, chunks, unroll):
    r, tq, dh = q_ref.shape
    _stream_softmax(q_ref[...].reshape(r * tq, dh), kt_ref, v_ref,
                    ((pa_ref, ra_ref), (pb_ref, rb_ref)), m_ref, acc_ref, chunks, unroll)
    for i in range(r):
        a = acc_ref[i * tq:(i + 1) * tq, :]
        o_ref[i] = (a[:, 0:HEAD_DIM] / a[:, HEAD_DIM:HEAD_DIM + 1]).astype(o_ref.dtype)


def _diff_kernel(q_ref, kt_ref, v_ref, lq1_ref, lk1_ref, lq2_ref, lk2_ref, gs_ref, o_ref,
                 pa_ref, ra_ref, pb_ref, rb_ref, m_ref, acc_ref, *, chunks, unroll, lam_init):
    r, tq, dh = q_ref.shape
    _stream_softmax(q_ref[...].reshape(r * tq, dh), kt_ref, v_ref,
                    ((pa_ref, ra_ref), (pb_ref, rb_ref)), m_ref, acc_ref, chunks, unroll)
    lam = (jnp.exp(jnp.sum(lq1_ref[...] * lk1_ref[...], axis=-1, keepdims=True))
           - jnp.exp(jnp.sum(lq2_ref[...] * lk2_ref[...], axis=-1, keepdims=True)) + lam_init)
    a1 = acc_ref[0:tq, :]
    a2 = acc_ref[tq:2 * tq, :]
    o = (a1[:, 0:HEAD_DIM] / a1[:, HEAD_DIM:HEAD_DIM + 1]
         - lam * (a2[:, 0:HEAD_DIM] / a2[:, HEAD_DIM:HEAD_DIM + 1]))
    o = o * lax.rsqrt(jnp.mean(o * o, axis=-1, keepdims=True) + EPS) * gs_ref[...]
    o_ref[...] = (o * (1.0 - lam_init)).astype(o_ref.dtype)


def _stream_attention(q, kt, v, *, q_start, n_q, tq, key_block, chunks, unroll=CHUNK_UNROLL, diff=None):
    b, g, r, _, dh = q.shape
    klen, kidx = key_block
    q0 = q_start // tq
    p_shape = (r * tq, max(chunks[1], chunks[3]) if chunks[0] else chunks[3])
    row_shape = (r * tq, LANES)
    in_specs = [
        pl.BlockSpec((None, None, r, tq, dh), lambda bi, gi, i: (bi, gi, 0, q0 + i, 0)),
        pl.BlockSpec((None, None, dh, klen), lambda bi, gi, i: (bi, gi, 0, kidx)),
        pl.BlockSpec((None, None, klen, LANES), lambda bi, gi, i: (bi, gi, kidx, 0)),
    ]
    args = [q, kt, v]
    if diff is None:
        body = functools.partial(_gqa_kernel, chunks=chunks, unroll=unroll)
        out_shape = jax.ShapeDtypeStruct((b, g, r, n_q, HEAD_DIM), BF16)
        out_spec = pl.BlockSpec((None, None, r, tq, HEAD_DIM), lambda bi, gi, i: (bi, gi, 0, i, 0))
        name = "gqa_attention"
    else:
        lam_vecs, g_subln, lam_init = diff
        body = functools.partial(_diff_kernel, chunks=chunks, unroll=unroll, lam_init=lam_init)
        small = lambda n: pl.BlockSpec((1, n), lambda bi, gi, i: (0, 0))
        in_specs += [small(B_QK_DIM)] * 4 + [small(HEAD_DIM)]
        args += [lv[None, :] for lv in lam_vecs] + [g_subln[None, :]]
        out_shape = jax.ShapeDtypeStruct((b, g, n_q, HEAD_DIM), BF16)
        out_spec = pl.BlockSpec((None, None, tq, HEAD_DIM), lambda bi, gi, i: (bi, gi, i, 0))
        name = "diff_attention"
    return pl.pallas_call(
        body,
        grid=(b, g, n_q // tq),
        in_specs=in_specs,
        out_specs=out_spec,
        out_shape=out_shape,
        scratch_shapes=[pltpu.VMEM(p_shape, BF16), pltpu.VMEM(row_shape, F32),
                        pltpu.VMEM(p_shape, BF16), pltpu.VMEM(row_shape, F32),
                        pltpu.VMEM(row_shape, F32), pltpu.VMEM(row_shape, F32)],
        compiler_params=_params(3),
        name=name,
    )(*args)


def _window_kernel(q_ref, kt_ref, v_ref, sink_ref, o_ref, *, n_latent, n_ctx, band):
    r, tq, dh = q_ref.shape
    n = r * tq
    q = q_ref[...].reshape(n, dh)
    sink = jnp.concatenate(
        [jnp.broadcast_to(sink_ref[i:i + 1, 0:1] * LOG2E, (tq, 1)) for i in range(r)], axis=0)

    s_ctx = jnp.dot(q, kt_ref[:, n_latent:n_latent + n_ctx], preferred_element_type=F32)
    m = jnp.maximum(jnp.max(s_ctx, axis=-1, keepdims=True), sink)
    if band:
        span = tq + 2 * WINDOW
        i = pl.program_id(2)
        start = pl.multiple_of(jnp.clip(i * tq - WINDOW, 0, n_latent - span), LANES)
        s_loc = jnp.dot(q, kt_ref[:, pl.ds(start, span)], preferred_element_type=F32)
        qpos = i * tq + lax.broadcasted_iota(jnp.int32, (n, span), 0) % tq
        kpos = start + lax.broadcasted_iota(jnp.int32, (n, span), 1)
        s_loc = jnp.where(jnp.abs(qpos - kpos) <= WINDOW, s_loc, NEG_INF)
        m = jnp.maximum(m, jnp.max(s_loc, axis=-1, keepdims=True))
    acc = jnp.dot(jnp.exp2(s_ctx - m).astype(BF16), v_ref[n_latent:n_latent + n_ctx, :],
                  preferred_element_type=F32)
    if band:
        acc = acc + jnp.dot(jnp.exp2(s_loc - m).astype(BF16), v_ref[pl.ds(start, span), :],
                            preferred_element_type=F32)
    denom = acc[:, HEAD_DIM:HEAD_DIM + 1] + jnp.exp2(sink - m)
    o = acc[:, 0:HEAD_DIM] / denom
    for i in range(r):
        o_ref[i] = o[i * tq:(i + 1) * tq].astype(o_ref.dtype)


def _window_attention(q, kt, v, sink_gr, *, q_start, n_q, tq, n_latent, n_ctx, band):
    b, g, r, stot, dh = q.shape
    q0 = q_start // tq
    sink = jnp.broadcast_to(sink_gr[:, :, None], (g, r, LANES)).astype(F32)
    return pl.pallas_call(
        functools.partial(_window_kernel, n_latent=n_latent, n_ctx=n_ctx, band=band),
        grid=(b, g, n_q // tq),
        in_specs=[
            pl.BlockSpec((None, None, r, tq, dh), lambda bi, gi, i: (bi, gi, 0, q0 + i, 0)),
            pl.BlockSpec((None, None, dh, stot), lambda bi, gi, i: (bi, gi, 0, 0)),
            pl.BlockSpec((None, None, stot, LANES), lambda bi, gi, i: (bi, gi, 0, 0)),
            pl.BlockSpec((None, r, LANES), lambda bi, gi, i: (gi, 0, 0)),
        ],
        out_specs=pl.BlockSpec((None, None, r, tq, HEAD_DIM), lambda bi, gi, i: (bi, gi, 0, i, 0)),
        out_shape=jax.ShapeDtypeStruct((b, g, r, n_q, HEAD_DIM), BF16),
        compiler_params=_params(3),
        name="window_attention",
    )(q, kt, v, sink)


def _post_kernel(x_ref, a_ref, mod_ref, wo_ref, g_ref, w1_ref, w3_ref, w2_ref, *rest, final):
    if final:
        gf_ref, o_ref = rest
    else:
        (o_ref,) = rest
    d = x_ref.shape[-1]
    mod = mod_ref[...]
    y = x_ref[...] + mod[:, 2 * d:3 * d] * jnp.dot(a_ref[...], wo_ref[...], preferred_element_type=F32)
    h = y * lax.rsqrt(jnp.mean(y * y, axis=-1, keepdims=True) + EPS) * g_ref[...]
    h = (h * (1.0 + mod[:, 4 * d:5 * d]) + mod[:, 3 * d:4 * d]).astype(BF16)
    u = jnp.dot(h, w1_ref[...], preferred_element_type=F32)
    v = jnp.dot(h, w3_ref[...], preferred_element_type=F32)
    act = (u * jax.nn.sigmoid(u) * v).astype(BF16)
    out = y + mod[:, 5 * d:6 * d] * jnp.dot(act, w2_ref[...], preferred_element_type=F32)
    if final:
        out = out * lax.rsqrt(jnp.mean(out * out, axis=-1, keepdims=True) + EPS) * gf_ref[...]
    o_ref[...] = out


def _post_attention(xc, attn, mod, w_out, g_ffn, w1, w3, w2, n_latent, g_final=None):
    b, stot, d = xc.shape
    t = TOKEN_TILE
    n_out = attn.shape[1]
    n_lat_tiles = n_latent // t
    n_ctx_row = mod.shape[0] - 1
    dff = w1.shape[1]
    final = g_final is not None
    const2 = lambda bi, ti: (0, 0)
    resident = lambda shape: pl.BlockSpec(shape, const2, pipeline_mode=pl.Buffered(1))
    tok3 = lambda bi, ti: (bi, ti, 0)
    in_specs = [
        pl.BlockSpec((None, t, d), tok3),
        pl.BlockSpec((None, t, attn.shape[-1]), tok3),
        pl.BlockSpec((None, 1, mod.shape[-1]),
                     lambda bi, ti: (jnp.where(ti < n_lat_tiles, bi, n_ctx_row), 0, 0)),
        resident((w_out.shape[0], d)),
        pl.BlockSpec((1, d), const2),
        resident((d, dff)), resident((d, dff)), resident((dff, d)),
    ]
    args = [xc, attn, mod, w_out, g_ffn[None, :], w1, w3, w2]
    if final:
        in_specs.append(pl.BlockSpec((1, d), const2))
        args.append(g_final[None, :])
    return pl.pallas_call(
        functools.partial(_post_kernel, final=final),
        grid=(b, n_out // t),
        in_specs=in_specs,
        out_specs=pl.BlockSpec((None, t, d), tok3),
        out_shape=jax.ShapeDtypeStruct((b, n_out, d), F32),
        compiler_params=_params(2),
        name="post_attention",
    )(*args)


def _merge(oa, ob, oc):
    b, _, _, n, dh = oa.shape
    flat = lambda o: o.reshape(b, -1, n, dh)
    o = jnp.concatenate([flat(oa), ob, flat(oc)], axis=1)
    return o.transpose(0, 2, 1, 3).reshape(b, n, -1)


def kernel(x, c, ctx, c_ctx, w_ada, b_ada, g_attn, g_ffn, w_in, g_q, g_k, lam_q1, lam_k1, lam_q2, lam_k2,
           g_subln, sink_logit, w_out, w_ff1, w_ff3, w_ff2, g_final):
    b, s, d = x.shape
    n_ctx = ctx.shape[1]
    depth = w_ada.shape[0]
    stot = s + n_ctx
    assert s % Q_TILE == 0 and s % K_TILE == 0 and s % n_ctx == 0 and n_ctx % TOKEN_TILE == 0
    assert s % GRID_W == 0 and b + 1 <= 8

    cvec = jnp.zeros((8, d), F32).at[:b].set(c).at[b].set(c_ctx)
    mods = _ada_modulation(cvec, w_ada, b_ada)[:, :b + 1, None, :]
    tabs = _rope_tables(s, n_ctx, HEAD_DIM) + _rope_tables(s, n_ctx, B_QK_DIM)

    xc = jnp.concatenate([x, ctx], axis=1)
    lat_chunks = (s // K_TILE, K_TILE, s, n_ctx)
    ctx_chunks = (0, K_TILE, 0, n_ctx)
    lat = dict(q_start=0, n_q=s, tq=Q_TILE, key_block=(stot, 0), chunks=lat_chunks)
    cq = dict(q_start=s, n_q=n_ctx, tq=n_ctx, key_block=(n_ctx, s // n_ctx), chunks=ctx_chunks)

    for layer in range(depth):
        last = layer == depth - 1
        lam_init = 0.8 - 0.6 * math.exp(-0.3 * layer)
        mod = mods[layer]
        qa, kat, va, qb, kbt, vb, qc, kct, vc = _input_projection(
            xc, mod, g_attn[layer], w_in[layer].astype(BF16), tabs, g_q[layer], g_k[layer], s)
        diff = ((lam_q1[layer], lam_k1[layer], lam_q2[layer], lam_k2[layer]), g_subln[layer], lam_init)
        sink_gr = sink_logit[layer].reshape(C_KV, C_HEADS // C_KV)
        win = dict(n_latent=s, n_ctx=n_ctx)

        oa = _stream_attention(qa, kat, va, **lat)
        ob = _stream_attention(qb, kbt, vb, diff=diff, **{**lat, 'tq': Q_TILE_DIFF, 'unroll': CHUNK_UNROLL_DIFF})
        oc = _window_attention(qc, kct, vc, sink_gr, q_start=0, n_q=s, tq=Q_TILE_WINDOW, band=True, **win)
        attn = _merge(oa, ob, oc)
        if not last:
            oa_c = _stream_attention(qa, kat, va, **cq)
            ob_c = _stream_attention(qb, kbt, vb, diff=diff, **cq)
            oc_c = _window_attention(qc, kct, vc, sink_gr, q_start=s, n_q=n_ctx, tq=n_ctx, band=False, **win)
            attn = jnp.concatenate([attn, _merge(oa_c, ob_c, oc_c)], axis=1)

        xc = _post_attention(xc, attn, mod, w_out[layer].astype(BF16), g_ffn[layer],
                             w_ff1[layer].astype(BF16), w_ff3[layer].astype(BF16), w_ff2[layer].astype(BF16),
                             s, g_final if last else None)
    return xc
```

```python
import functools
import math

import jax
import jax.numpy as jnp
from jax import lax
from jax.experimental import pallas as pl
from jax.experimental.pallas import tpu as pltpu

F32 = jnp.float32
BF16 = jnp.bfloat16

GRID_W = 64
HEAD_DIM = 64
A_HEADS, A_KV = 6, 2
B_HEADS = 4
B_QK_DIM = HEAD_DIM // 2
C_HEADS, C_KV = 6, 2
WINDOW = 128
ROPE_THETA = 10000.0
EPS = 1e-6
NEG_INF = -1e30
LOG2E = 1.4426950408889634

LANES = 128
SUBLANES = 8
ADA_COLUMN_BLOCKS = 4
VMEM_LIMIT = 56 * 1024 * 1024
TOKEN_TILE = 256
Q_TILE = 1024
Q_TILE_DIFF = 512
Q_TILE_WINDOW = 256
K_TILE = 256
CHUNK_UNROLL = 16
CHUNK_UNROLL_DIFF = 32


def _params(n_axes):
    return pltpu.CompilerParams(dimension_semantics=("arbitrary",) * n_axes,
                                vmem_limit_bytes=VMEM_LIMIT)


def _token_specs(t, d, n_lat_tiles):
    lat = pl.BlockSpec((None, t, d), lambda bi, ti: (bi, jnp.minimum(ti, n_lat_tiles - 1), 0))
    ctx = pl.BlockSpec((None, t, d), lambda bi, ti: (bi, jnp.maximum(ti - n_lat_tiles, 0), 0))
    return lat, ctx


def _ada_kernel(cv_ref, w_ref, b_ref, o_ref):
    cv = cv_ref[...]
    s = cv * jax.nn.sigmoid(cv)
    o_ref[...] = jnp.dot(s, w_ref[...], preferred_element_type=F32,
                         precision=lax.Precision.HIGHEST) + b_ref[...]


def _ada_modulation(cvec, w_ada, b_ada):
    depth, d, n = w_ada.shape
    rows = cvec.shape[0]
    bn = n // ADA_COLUMN_BLOCKS
    return pl.pallas_call(
        _ada_kernel,
        grid=(depth, ADA_COLUMN_BLOCKS),
        in_specs=[pl.BlockSpec((rows, d), lambda l, j: (0, 0)),
                  pl.BlockSpec((None, d, bn), lambda l, j: (l, 0, j)),
                  pl.BlockSpec((None, 1, bn), lambda l, j: (l, 0, j))],
        out_specs=pl.BlockSpec((None, rows, bn), lambda l, j: (l, 0, j)),
        out_shape=jax.ShapeDtypeStruct((depth, rows, n), F32),
        compiler_params=_params(2),
        name="ada_modulation",
    )(cvec, w_ada, b_ada.reshape(depth, 1, n))


def _rope_tables(s, c, head_dim):
    half = head_dim // 4
    t = jnp.arange(s, dtype=jnp.int32)
    rows = (t // GRID_W).astype(F32)
    cols = (t % GRID_W).astype(F32)
    freqs = ROPE_THETA ** (-jnp.arange(half, dtype=F32) / half)
    ar = rows[:, None] * freqs[None, :]
    ac = cols[:, None] * freqs[None, :]
    cos = jnp.concatenate([jnp.cos(ar)] * 2 + [jnp.cos(ac)] * 2, axis=-1)
    sin = jnp.concatenate([-jnp.sin(ar), jnp.sin(ar), -jnp.sin(ac), jnp.sin(ac)], axis=-1)
    cos = jnp.concatenate([cos, jnp.ones((c, head_dim), F32)], axis=0)
    sin = jnp.concatenate([sin, jnp.zeros((c, head_dim), F32)], axis=0)
    reps = LANES // head_dim
    return jnp.tile(cos, (1, reps)), jnp.tile(sin, (1, reps))


def _inproj_kernel(x_ref, ctx_ref, mod_ref, g_ref, w_ref, cos64_ref, sin64_ref, cos32_ref, sin32_ref,
                   gq_ref, gk_ref,
                   qa_ref, kat_ref, va_ref, qb_ref, kbt_ref, vb_ref, qc_ref, kct_ref, vc_ref,
                   *, n_lat_tiles):
    t, d = x_ref.shape
    x = jnp.where(pl.program_id(1) >= n_lat_tiles, ctx_ref[...], x_ref[...])
    mod = mod_ref[...]
    shift, scale = mod[:, 0:d], mod[:, d:2 * d]
    y = x * lax.rsqrt(jnp.mean(x * x, axis=-1, keepdims=True) + EPS) * g_ref[...]
    h = (y * (1.0 + scale) + shift).astype(BF16)
    proj = jnp.dot(h, w_ref[...], preferred_element_type=F32)

    lane = lax.broadcasted_iota(jnp.int32, (t, LANES), 1)
    low_half = lane < HEAD_DIM
    cos64, sin64 = cos64_ref[...], sin64_ref[...]
    cos32, sin32 = cos32_ref[...], sin32_ref[...]

    def slab(k):
        return proj[:, k * LANES:(k + 1) * LANES]

    def rope(v, cos, sin, half):
        fwd = pltpu.roll(v, LANES - half, 1)
        bwd = pltpu.roll(v, half, 1)
        partner = jnp.where((lane % (2 * half)) < half, fwd, bwd)
        return v * cos + partner * sin

    def head_norm(v, g):
        sq = v * v
        lo = jnp.sum(jnp.where(low_half, sq, 0.0), axis=-1, keepdims=True)
        hi = jnp.sum(jnp.where(low_half, 0.0, sq), axis=-1, keepdims=True)
        ms = jnp.where(low_half, lo, hi) * (1.0 / HEAD_DIM)
        return v * lax.rsqrt(ms + EPS) * g

    def heads_of(v):
        return v[:, 0:HEAD_DIM], v[:, HEAD_DIM:LANES]

    def with_ones(v):
        return jnp.where(low_half, v, jnp.where(lane == HEAD_DIM, 1.0, 0.0))

    def store_v(ref, idx, v):
        ref[idx] = with_ones(v).astype(BF16)
        ref[idx + 1] = with_ones(pltpu.roll(v, HEAD_DIM, 1)).astype(BF16)

    def store_kt(ref, idx, v):
        vt = v.T
        ref[idx] = vt[0:HEAD_DIM].astype(BF16)
        ref[idx + 1] = vt[HEAD_DIM:LANES].astype(BF16)

    sa = HEAD_DIM ** -0.5 * LOG2E
    for k in range(3):
        q = rope(head_norm(slab(k), gq_ref[...]), cos64, sin64, HEAD_DIM // 4) * sa
        for j, qh in enumerate(heads_of(q)):
            hd = 2 * k + j
            qa_ref[hd // 3, hd % 3] = qh.astype(BF16)
    store_kt(kat_ref, 0, rope(head_norm(slab(3), gk_ref[...]), cos64, sin64, HEAD_DIM // 4))
    store_v(va_ref, 0, slab(4))

    sb = B_QK_DIM ** -0.5 * LOG2E
    first_map = (lane % HEAD_DIM) < B_QK_DIM
    for k in range(2):
        q = rope(slab(5 + k), cos32, sin32, B_QK_DIM // 4) * sb
        q1 = jnp.where(first_map, q, 0.0)
        q2 = jnp.where(first_map, 0.0, q)
        for j in range(2):
            qb_ref[2 * k + j, 0] = heads_of(q1)[j].astype(BF16)
            qb_ref[2 * k + j, 1] = heads_of(q2)[j].astype(BF16)
        store_kt(kbt_ref, 2 * k, rope(slab(7 + k), cos32, sin32, B_QK_DIM // 4))
        store_v(vb_ref, 2 * k, slab(9 + k))

    sc = HEAD_DIM ** -0.5 * LOG2E
    for k in range(3):
        q = rope(slab(11 + k), cos64, sin64, HEAD_DIM // 4) * sc
        for j, qh in enumerate(heads_of(q)):
            hd = 2 * k + j
            qc_ref[hd // 3, hd % 3] = qh.astype(BF16)
    store_kt(kct_ref, 0, rope(slab(14), cos64, sin64, HEAD_DIM // 4))
    store_v(vc_ref, 0, slab(15))


def _input_projection(x, ctx, mod, g_attn, w_in, tabs, g_q, g_k):
    b, n_latent, d = x.shape
    stot = n_latent + ctx.shape[1]
    t = TOKEN_TILE
    n_lat_tiles = n_latent // t
    n_ctx_row = mod.shape[0] - 1
    in_width = w_in.shape[1]
    cos64, sin64, cos32, sin32 = tabs
    gq = jnp.tile(g_q, 2)[None, :]
    gk = jnp.tile(g_k, 2)[None, :]

    tok = lambda bi, ti: (ti, 0)
    const2 = lambda bi, ti: (0, 0)
    qshape = lambda g, r: jax.ShapeDtypeStruct((b, g, r, stot, HEAD_DIM), BF16)
    ktshape = lambda g: jax.ShapeDtypeStruct((b, g, HEAD_DIM, stot), BF16)
    vshape = lambda g: jax.ShapeDtypeStruct((b, g, stot, LANES), BF16)
    qspec = lambda g, r: pl.BlockSpec((None, g, r, t, HEAD_DIM), lambda bi, ti: (bi, 0, 0, ti, 0))
    ktspec = lambda g: pl.BlockSpec((None, g, HEAD_DIM, t), lambda bi, ti: (bi, 0, 0, ti))
    vspec = lambda g: pl.BlockSpec((None, g, t, LANES), lambda bi, ti: (bi, 0, ti, 0))

    return pl.pallas_call(
        functools.partial(_inproj_kernel, n_lat_tiles=n_lat_tiles),
        grid=(b, stot // t),
        in_specs=[
            *_token_specs(t, d, n_lat_tiles),
            pl.BlockSpec((None, 1, mod.shape[-1]),
                         lambda bi, ti: (jnp.where(ti < n_lat_tiles, bi, n_ctx_row), 0, 0)),
            pl.BlockSpec((1, d), const2),
            pl.BlockSpec((d, in_width), const2),
            pl.BlockSpec((t, LANES), tok), pl.BlockSpec((t, LANES), tok),
            pl.BlockSpec((t, LANES), tok), pl.BlockSpec((t, LANES), tok),
            pl.BlockSpec((1, LANES), const2), pl.BlockSpec((1, LANES), const2),
        ],
        out_specs=[qspec(A_KV, A_HEADS // A_KV), ktspec(A_KV), vspec(A_KV),
                   qspec(B_HEADS, 2), ktspec(B_HEADS), vspec(B_HEADS),
                   qspec(C_KV, C_HEADS // C_KV), ktspec(C_KV), vspec(C_KV)],
        out_shape=[qshape(A_KV, A_HEADS // A_KV), ktshape(A_KV), vshape(A_KV),
                   qshape(B_HEADS, 2), ktshape(B_HEADS), vshape(B_HEADS),
                   qshape(C_KV, C_HEADS // C_KV), ktshape(C_KV), vshape(C_KV)],
        compiler_params=_params(2),
        name="input_projection",
    )(x, ctx, mod, g_attn[None, :], w_in, cos64, sin64, cos32, sin32, gq, gk)


def _stream_softmax(q, kt_ref, v_ref, bufs, m_ref, acc_ref, chunks, unroll):
    n_main, tk, tail_start, tail_size = chunks
    m_ref[...] = jnp.full(m_ref.shape, NEG_INF, F32)
    acc_ref[...] = jnp.zeros(acc_ref.shape, F32)

    def probs(off, size, p_ref, r_ref):
        s = jnp.dot(q, kt_ref[:, pl.ds(off, size)], preferred_element_type=F32)
        m_prev = m_ref[...]
        m_new = jnp.maximum(m_prev, jnp.max(s, axis=-1, keepdims=True))
        p_ref[:, 0:size] = jnp.concatenate(
            [jnp.exp2(s[:, c * LANES:(c + 1) * LANES] - m_new) for c in range(size // LANES)],
            axis=1).astype(BF16)
        r_ref[...] = jnp.exp2(m_prev - m_new)
        m_ref[...] = m_new

    def accumulate(off, size, p_ref, r_ref):
        pv = jnp.dot(p_ref[:, 0:size], v_ref[pl.ds(off, size), :], preferred_element_type=F32)
        acc_ref[...] = r_ref[...] * acc_ref[...] + pv

    trips = (n_main - 1) // unroll if n_main else 0
    if n_main:
        probs(0, tk, *bufs[0])

    def group(g, carry):
        base = g * (unroll * tk)
        for u in range(unroll):
            probs(pl.multiple_of(base + (u + 1) * tk, LANES), tk, *bufs[(u + 1) % 2])
            accumulate(pl.multiple_of(base + u * tk, LANES), tk, *bufs[u % 2])
        return carry

    if trips:
        lax.fori_loop(0, trips, group, 0)
    for c in range(trips * unroll, n_main):
        if c + 1 < n_main:
            probs((c + 1) * tk, tk, *bufs[(c + 1) % 2])
        else:
            probs(tail_start, tail_size, *bufs[(c + 1) % 2])
        accumulate(c * tk, tk, *bufs[c % 2])
    if not n_main:
        probs(tail_start, tail_size, *bufs[0])
    accumulate(tail_start, tail_size, *bufs[n_main % 2])


def _gqa_kernel(q_ref, kt_ref, v_ref, *rest, chunks, unroll):
    o_ref, pa_ref, ra_ref, pb_ref, rb_ref, m_ref, acc_ref = rest[-7:]
    r, tq, dh = q_ref.shape
    _stream_softmax(q_ref[...].reshape(r * tq, dh), kt_ref, v_ref,
                    ((pa_ref, ra_ref), (pb_ref, rb_ref)), m_ref, acc_ref, chunks, unroll)
    for i in range(r):
        a = acc_ref[i * tq:(i + 1) * tq, :]
        o_ref[i] = (a[:, 0:HEAD_DIM] / a[:, HEAD_DIM:HEAD_DIM + 1]).astype(o_ref.dtype)


def _diff_kernel(q_ref, kt_ref, v_ref, lq1_ref, lk1_ref, lq2_ref, lk2_ref, gs_ref, *rest,
                 chunks, unroll, lam_init):
    o_ref, pa_ref, ra_ref, pb_ref, rb_ref, m_ref, acc_ref = rest[-7:]
    r, tq, dh = q_ref.shape
    _stream_softmax(q_ref[...].reshape(r * tq, dh), kt_ref, v_ref,
                    ((pa_ref, ra_ref), (pb_ref, rb_ref)), m_ref, acc_ref, chunks, unroll)
    lam = (jnp.exp(jnp.sum(lq1_ref[...] * lk1_ref[...], axis=-1, keepdims=True))
           - jnp.exp(jnp.sum(lq2_ref[...] * lk2_ref[...], axis=-1, keepdims=True)) + lam_init)
    a1 = acc_ref[0:tq, :]
    a2 = acc_ref[tq:2 * tq, :]
    o = (a1[:, 0:HEAD_DIM] / a1[:, HEAD_DIM:HEAD_DIM + 1]
         - lam * (a2[:, 0:HEAD_DIM] / a2[:, HEAD_DIM:HEAD_DIM + 1]))
    o = o * lax.rsqrt(jnp.mean(o * o, axis=-1, keepdims=True) + EPS) * gs_ref[...]
    o_ref[...] = (o * (1.0 - lam_init)).astype(o_ref.dtype)


def _stream_attention(q, kt, v, *, q_start, n_q, tq, key_block, chunks, unroll=CHUNK_UNROLL, diff=None):
    b, g, r, stot, dh = q.shape
    klen, kidx = key_block
    q0 = q_start // tq
    p_shape = (r * tq, max(chunks[1], chunks[3]) if chunks[0] else chunks[3])
    row_shape = (r * tq, LANES)
    in_specs = [
        pl.BlockSpec((None, None, r, tq, dh), lambda bi, gi, i: (bi, gi, 0, q0 + i, 0)),
        pl.BlockSpec((None, None, dh, klen), lambda bi, gi, i: (bi, gi, 0, kidx)),
        pl.BlockSpec((None, None, klen, LANES), lambda bi, gi, i: (bi, gi, kidx, 0)),
    ]
    args = [q, kt, v]
    if diff is None:
        body = functools.partial(_gqa_kernel, chunks=chunks, unroll=unroll)
        out_shape = jax.ShapeDtypeStruct((b, g, r, n_q, HEAD_DIM), BF16)
        out_spec = pl.BlockSpec((None, None, r, tq, HEAD_DIM), lambda bi, gi, i: (bi, gi, 0, i, 0))
        name = "gqa_attention"
    else:
        lam_vecs, g_subln, lam_init = diff
        body = functools.partial(_diff_kernel, chunks=chunks, unroll=unroll, lam_init=lam_init)
        small = lambda n: pl.BlockSpec((1, n), lambda bi, gi, i: (0, 0))
        in_specs += [small(B_QK_DIM)] * 4 + [small(HEAD_DIM)]
        args += [lv[None, :] for lv in lam_vecs] + [g_subln[None, :]]
        out_shape = jax.ShapeDtypeStruct((b, g, n_q, HEAD_DIM), BF16)
        out_spec = pl.BlockSpec((None, None, tq, HEAD_DIM), lambda bi, gi, i: (bi, gi, i, 0))
        name = "diff_attention"
    return pl.pallas_call(
        body,
        grid=(b, g, n_q // tq),
        in_specs=in_specs,
        out_specs=out_spec,
        out_shape=out_shape,
        scratch_shapes=[pltpu.VMEM(p_shape, BF16), pltpu.VMEM(row_shape, F32),
                        pltpu.VMEM(p_shape, BF16), pltpu.VMEM(row_shape, F32),
                        pltpu.VMEM(row_shape, F32), pltpu.VMEM(row_shape, F32)],
        compiler_params=_params(3),
        name=name,
    )(*args)


def _window_kernel(q_ref, kt_ref, v_ref, sink_ref, *rest, n_latent, n_ctx, band):
    o_ref = rest[-1]
    r, tq, dh = q_ref.shape
    n = r * tq
    q = q_ref[...].reshape(n, dh)
    sink = jnp.concatenate(
        [jnp.broadcast_to(sink_ref[i:i + 1, :] * LOG2E, (tq, LANES)) for i in range(r)], axis=0)

    def probabilities(s, m):
        return jnp.concatenate([jnp.exp2(s[:, c * LANES:(c + 1) * LANES] - m)
                                for c in range(s.shape[1] // LANES)], axis=1).astype(BF16)

    s_ctx = jnp.dot(q, kt_ref[:, n_latent:n_latent + n_ctx], preferred_element_type=F32)
    m = jnp.maximum(sink, jnp.max(s_ctx, axis=-1, keepdims=True))
    if band:
        span = tq + 2 * WINDOW
        i = pl.program_id(2)
        start = pl.multiple_of(jnp.clip(i * tq - WINDOW, 0, n_latent - span), LANES)
        dist = (lax.broadcasted_iota(jnp.int32, (tq, span), 0)
                - lax.broadcasted_iota(jnp.int32, (tq, span), 1) + (i * tq - start))
        valid = (jnp.abs(dist) <= WINDOW)[None]
        s_loc = jnp.dot(q, kt_ref[:, pl.ds(start, span)], preferred_element_type=F32)
        s_loc = jnp.where(valid, s_loc.reshape(r, tq, span), NEG_INF).reshape(n, span)
        m = jnp.maximum(m, jnp.max(s_loc, axis=-1, keepdims=True))
    acc = jnp.dot(probabilities(s_ctx, m), v_ref[n_latent:n_latent + n_ctx, :], preferred_element_type=F32)
    if band:
        acc = acc + jnp.dot(probabilities(s_loc, m), v_ref[pl.ds(start, span), :],
                            preferred_element_type=F32)
    denom = acc[:, HEAD_DIM:HEAD_DIM + 1] + jnp.exp2(sink - m)[:, 0:1]
    o = acc[:, 0:HEAD_DIM] / denom
    for i in range(r):
        o_ref[i] = o[i * tq:(i + 1) * tq].astype(o_ref.dtype)


def _window_attention(q, kt, v, sink_gr, *, q_start, n_q, tq, n_latent, n_ctx, band):
    b, g, r, stot, dh = q.shape
    q0 = q_start // tq
    sink = jnp.broadcast_to(sink_gr[:, :, None], (g, r, LANES)).astype(F32)
    in_specs = [
        pl.BlockSpec((None, None, r, tq, dh), lambda bi, gi, i: (bi, gi, 0, q0 + i, 0)),
        pl.BlockSpec((None, None, dh, stot), lambda bi, gi, i: (bi, gi, 0, 0)),
        pl.BlockSpec((None, None, stot, LANES), lambda bi, gi, i: (bi, gi, 0, 0)),
        pl.BlockSpec((None, r, LANES), lambda bi, gi, i: (gi, 0, 0)),
    ]
    return pl.pallas_call(
        functools.partial(_window_kernel, n_latent=n_latent, n_ctx=n_ctx, band=band),
        grid=(b, g, n_q // tq),
        in_specs=in_specs,
        out_specs=pl.BlockSpec((None, None, r, tq, HEAD_DIM), lambda bi, gi, i: (bi, gi, 0, i, 0)),
        out_shape=jax.ShapeDtypeStruct((b, g, r, n_q, HEAD_DIM), BF16),
        compiler_params=_params(3),
        name="window_attention",
    )(q, kt, v, sink)


def _post_kernel(*refs, n_lat_tiles, final):
    def heads_of(oa_ref, ob_ref, oc_ref):
        return ([oa_ref[g, r] for g in range(oa_ref.shape[0]) for r in range(oa_ref.shape[1])]
                + [ob_ref[h] for h in range(ob_ref.shape[0])]
                + [oc_ref[g, r] for g in range(oc_ref.shape[0]) for r in range(oc_ref.shape[1])])

    if final:
        x_ref, oa_ref, ob_ref, oc_ref, mod_ref, wo_ref, g_ref, w1_ref, w3_ref, w2_ref, gf_ref, ox_ref = refs
        x = x_ref[...]
        heads = heads_of(oa_ref, ob_ref, oc_ref)
    else:
        (x_ref, ctx_ref, oa_ref, ob_ref, oc_ref, oa_ctx_ref, ob_ctx_ref, oc_ctx_ref,
         mod_ref, wo_ref, g_ref, w1_ref, w3_ref, w2_ref, ox_ref, octx_ref) = refs
        is_ctx = pl.program_id(1) >= n_lat_tiles
        x = jnp.where(is_ctx, ctx_ref[...], x_ref[...])
        heads = [jnp.where(is_ctx, c, l) for c, l in zip(heads_of(oa_ctx_ref, ob_ctx_ref, oc_ctx_ref),
                                                       heads_of(oa_ref, ob_ref, oc_ref))]
    d = x.shape[-1]
    mod = mod_ref[...]
    attn = jnp.concatenate(heads, axis=-1)
    y = x + mod[:, 2 * d:3 * d] * jnp.dot(attn, wo_ref[...], preferred_element_type=F32)
    h = y * lax.rsqrt(jnp.mean(y * y, axis=-1, keepdims=True) + EPS) * g_ref[...]
    h = (h * (1.0 + mod[:, 4 * d:5 * d]) + mod[:, 3 * d:4 * d]).astype(BF16)
    u = jnp.dot(h, w1_ref[...], preferred_element_type=F32)
    v = jnp.dot(h, w3_ref[...], preferred_element_type=F32)
    act = (u * jax.nn.sigmoid(u) * v).astype(BF16)
    out = y + mod[:, 5 * d:6 * d] * jnp.dot(act, w2_ref[...], preferred_element_type=F32)
    if final:
        ox_ref[...] = out * lax.rsqrt(jnp.mean(out * out, axis=-1, keepdims=True) + EPS) * gf_ref[...]
    else:
        @pl.when(is_ctx)
        def _():
            octx_ref[...] = out

        @pl.when(jnp.logical_not(is_ctx))
        def _():
            ox_ref[...] = out


def _post_attention(x, ctx, heads, heads_ctx, mod, w_out, g_ffn, w1, w3, w2, g_final=None):
    oa, ob, oc = heads
    b, n_latent, d = x.shape
    n_ctx = ctx.shape[1]
    t = TOKEN_TILE
    n_lat_tiles = n_latent // t
    n_ctx_row = mod.shape[0] - 1
    dff = w1.shape[1]
    final = g_final is not None
    const2 = lambda bi, ti: (0, 0)
    resident = lambda shape: pl.BlockSpec(shape, const2, pipeline_mode=pl.Buffered(1))
    lat_spec, ctx_spec = _token_specs(t, d, n_lat_tiles)
    lat_tile = lambda ti: jnp.minimum(ti, n_lat_tiles - 1)
    ctx_tile = lambda ti: jnp.maximum(ti - n_lat_tiles, 0)

    def head_specs(tile):
        grouped = lambda a: pl.BlockSpec((None,) + a.shape[1:3] + (t, HEAD_DIM),
                                         lambda bi, ti: (bi, 0, 0, tile(ti), 0))
        return [grouped(oa), pl.BlockSpec((None, ob.shape[1], t, HEAD_DIM), lambda bi, ti: (bi, 0, tile(ti), 0)),
                grouped(oc)]

    in_specs = [lat_spec] + ([] if final else [ctx_spec]) + head_specs(lat_tile) + (
        [] if final else head_specs(ctx_tile)) + [
        pl.BlockSpec((None, 1, mod.shape[-1]),
                     lambda bi, ti: (jnp.where(ti < n_lat_tiles, bi, n_ctx_row), 0, 0)),
        resident((w_out.shape[0], d)),
        pl.BlockSpec((1, d), const2),
        resident((d, dff)), resident((d, dff)), resident((dff, d)),
    ]
    args = ([x] + ([] if final else [ctx]) + list(heads) + ([] if final else list(heads_ctx))
            + [mod, w_out, g_ffn[None, :], w1, w3, w2])
    if final:
        in_specs.append(pl.BlockSpec((1, d), const2))
        args.append(g_final[None, :])
        out_specs = lat_spec
        out_shape = jax.ShapeDtypeStruct((b, n_latent, d), F32)
        tiles = n_lat_tiles
    else:
        out_specs = [lat_spec, ctx_spec]
        out_shape = [jax.ShapeDtypeStruct((b, n_latent, d), F32), jax.ShapeDtypeStruct((b, n_ctx, d), F32)]
        tiles = n_lat_tiles + n_ctx // t
    return pl.pallas_call(
        functools.partial(_post_kernel, n_lat_tiles=n_lat_tiles, final=final),
        grid=(b, tiles),
        in_specs=in_specs,
        out_specs=out_specs,
        out_shape=out_shape,
        compiler_params=_params(2),
        name="post_attention",
    )(*args)


def kernel(x, c, ctx, c_ctx, w_ada, b_ada, g_attn, g_ffn, w_in, g_q, g_k, lam_q1, lam_k1, lam_q2, lam_k2,
           g_subln, sink_logit, w_out, w_ff1, w_ff3, w_ff2, g_final):
    b, s, d = x.shape
    n_ctx = ctx.shape[1]
    depth = w_ada.shape[0]
    stot = s + n_ctx
    assert s % Q_TILE == 0 and s % K_TILE == 0 and s % n_ctx == 0 and n_ctx % TOKEN_TILE == 0
    assert s % GRID_W == 0 and b + 1 <= SUBLANES

    cvec = jnp.zeros((SUBLANES, d), F32).at[:b].set(c).at[b].set(c_ctx)
    mods = _ada_modulation(cvec, w_ada, b_ada)[:, :b + 1, None, :]
    tabs = _rope_tables(s, n_ctx, HEAD_DIM) + _rope_tables(s, n_ctx, B_QK_DIM)

    lat_chunks = (s // K_TILE, K_TILE, s, n_ctx)
    ctx_chunks = (0, K_TILE, 0, n_ctx)
    lat = dict(q_start=0, n_q=s, tq=Q_TILE, key_block=(stot, 0), chunks=lat_chunks)
    cq = dict(q_start=s, n_q=n_ctx, tq=n_ctx, key_block=(n_ctx, s // n_ctx), chunks=ctx_chunks)

    for layer in range(depth):
        last = layer == depth - 1
        lam_init = 0.8 - 0.6 * math.exp(-0.3 * layer)
        mod = mods[layer]
        qa, kat, va, qb, kbt, vb, qc, kct, vc = _input_projection(
            x, ctx, mod, g_attn[layer], w_in[layer].astype(BF16), tabs, g_q[layer], g_k[layer])
        diff = ((lam_q1[layer], lam_k1[layer], lam_q2[layer], lam_k2[layer]), g_subln[layer], lam_init)
        sink_gr = sink_logit[layer].reshape(C_KV, C_HEADS // C_KV)
        win = dict(n_latent=s, n_ctx=n_ctx)

        heads = (_stream_attention(qa, kat, va, **lat),
                 _stream_attention(qb, kbt, vb, diff=diff,
                                   **{**lat, 'tq': Q_TILE_DIFF, 'unroll': CHUNK_UNROLL_DIFF}),
                 _window_attention(qc, kct, vc, sink_gr, q_start=0, n_q=s, tq=Q_TILE_WINDOW, band=True, **win))
        weights = (w_out[layer].astype(BF16), g_ffn[layer],
                   w_ff1[layer].astype(BF16), w_ff3[layer].astype(BF16), w_ff2[layer].astype(BF16))
        if last:
            return _post_attention(x, ctx, heads, None, mod, *weights, g_final=g_final)
        heads_ctx = (_stream_attention(qa, kat, va, **cq),
                     _stream_attention(qb, kbt, vb, diff=diff, **cq),
                     _window_attention(qc, kct, vc, sink_gr, q_start=s, n_q=n_ctx, tq=n_ctx, band=False, **win))
        x, ctx = _post_attention(x, ctx, heads, heads_ctx, mod, *weights)
```
